```python
import jax, jax.numpy as jnp
from jax import lax
import numpy as np

D_MODEL = 4096
BATCH = 1
SEQ = 8192
DEPTH = 1

D_A = D_MODEL // 2
CHUNK = 128
HD_A = 128
N_HEADS_A = D_A // HD_A
D_B = D_MODEL // 2
POOL_WINDOWS = (2, 4, 8, 16)
N_POOL_GROUPS = len(POOL_WINDOWS)
POOL_GD = D_B // N_POOL_GROUPS
N_BRANCHES = 2
D_FF = 4 * D_MODEL
N_MOD = 6
IN_COLS = 2 * D_A + D_B + N_BRANCHES * D_MODEL
EPS = 1e-6

kernel_name = "gated_gmlp_pool_hybrid_block"


def rmsnorm(x, g):
    xf = x.astype(jnp.float32)
    y = xf * lax.rsqrt(jnp.mean(xf * xf, axis=-1, keepdims=True) + EPS)
    return (y * g.astype(jnp.float32)).astype(x.dtype)


def layernorm(x, g, b):
    xf = x.astype(jnp.float32)
    mu = jnp.mean(xf, axis=-1, keepdims=True)
    var = jnp.mean(jnp.square(xf - mu), axis=-1, keepdims=True)
    y = (xf - mu) * lax.rsqrt(var + EPS)
    return (y * g.astype(jnp.float32) + b.astype(jnp.float32)).astype(x.dtype)


def modulate(h, shift, scale):
    return h * (1 + scale[:, None, :]) + shift[:, None, :]


def spatial_gating(u, v, ln_v_g, ln_v_b, w_spatial, b_spatial):
    B, S, _ = u.shape
    u = jax.nn.gelu(u)
    v = layernorm(jax.nn.gelu(v), ln_v_g, ln_v_b)
    n_chunks = S // CHUNK
    vb = v.reshape(B, n_chunks, CHUNK, N_HEADS_A, HD_A)
    mask = jnp.tril(jnp.ones((CHUNK, CHUNK), dtype=bool))
    w = jnp.where(mask[None], w_spatial, jnp.zeros_like(w_spatial))
    mixed = jnp.einsum('hts,bnshd->bnthd', w, vb)
    mixed = mixed + jnp.transpose(b_spatial)[None, None, :, :, None]
    return u * mixed.reshape(B, S, D_A)


def pool_mix(p, w_pool, b_pool, pool_scale):
    B, S, _ = p.shape
    pf = p.astype(jnp.float32).reshape(B, S, N_POOL_GROUPS, POOL_GD)
    cs = jnp.cumsum(pf, axis=1)
    win = jnp.array(POOL_WINDOWS, dtype=jnp.int32)
    t = jnp.arange(S, dtype=jnp.int32)[:, None]
    lo = t - win[None, :]
    gathered = cs[:, jnp.clip(lo, 0, None), jnp.arange(N_POOL_GROUPS)[None, :], :]
    lower = jnp.where((lo >= 0)[None, :, :, None], gathered, 0.0)
    count = jnp.minimum(t + 1, win[None, :]).astype(jnp.float32)
    mean = (cs - lower) / count[None, :, :, None]
    pooled = (mean - pf).astype(p.dtype)
    y = jnp.einsum('bsgc,gcd->bsgd', pooled, w_pool) + b_pool
    return y.reshape(B, S, D_B) * pool_scale


def token_mixer(h, w_in, ln_v_g, ln_v_b, w_spatial, b_spatial, w_pool, b_pool,
                pool_scale, b_gate, w_up_a, w_up_b, w_out):
    proj = jnp.einsum('bsd,de->bse', h, w_in)
    u, v, p, ga, gb = jnp.split(
        proj, [D_A, 2 * D_A, 2 * D_A + D_B, 2 * D_A + D_B + D_MODEL], axis=-1)
    y_a = spatial_gating(u, v, ln_v_g, ln_v_b, w_spatial, b_spatial)
    y_b = pool_mix(p, w_pool, b_pool, pool_scale)
    g_a = jax.nn.sigmoid(ga + b_gate[0])
    g_b = jax.nn.sigmoid(gb + b_gate[1])
    merged = (g_a * jnp.einsum('bsc,cd->bsd', y_a, w_up_a)
              + g_b * jnp.einsum('bsc,cd->bsd', y_b, w_up_b))
    return jnp.einsum('bsd,de->bse', merged, w_out)


def channel_mixer(h, w_ff1, w_ff2):
    a = jnp.einsum('bsd,df->bsf', h, w_ff1)
    return jnp.einsum('bsf,fd->bsd', jnp.square(jax.nn.relu(a)), w_ff2)


def setup_inputs(seed: int = 0) -> dict:
    key = jax.random.key(seed)
    ks = jax.random.split(key, 24)
    f32 = jnp.float32
    L = DEPTH

    def nrm(k, shape, scale):
        return jax.random.normal(k, shape, f32) * scale

    return {
        "x": nrm(ks[0], (BATCH, SEQ, D_MODEL), 1.0),
        "c": nrm(ks[1], (BATCH, D_MODEL), 1.0),
        "w_ada": nrm(ks[2], (L, D_MODEL, N_MOD * D_MODEL), 0.5 * D_MODEL ** -0.5),
        "b_ada": nrm(ks[3], (L, N_MOD * D_MODEL), 0.01),
        "norm1_g": 1.0 + nrm(ks[4], (L, D_MODEL), 0.05),
        "w_in": nrm(ks[5], (L, D_MODEL, IN_COLS), D_MODEL ** -0.5),
        "ln_v_g": 1.0 + nrm(ks[6], (L, D_A), 0.05),
        "ln_v_b": nrm(ks[7], (L, D_A), 0.02),
        "w_spatial": nrm(ks[8], (L, N_HEADS_A, CHUNK, CHUNK), CHUNK ** -0.5),
        "b_spatial": 1.0 + nrm(ks[9], (L, N_HEADS_A, CHUNK), 0.1),
        "w_pool": nrm(ks[10], (L, N_POOL_GROUPS, POOL_GD, POOL_GD), POOL_GD ** -0.5),
        "b_pool": nrm(ks[11], (L, N_POOL_GROUPS, POOL_GD), 0.02),
        "pool_scale": 1.0 + nrm(ks[12], (L, D_B), 0.1),
        "b_gate": nrm(ks[13], (L, N_BRANCHES, D_MODEL), 0.1),
        "w_up_a": nrm(ks[14], (L, D_A, D_MODEL), D_A ** -0.5),
        "w_up_b": nrm(ks[15], (L, D_B, D_MODEL), D_B ** -0.5),
        "w_out": nrm(ks[16], (L, D_MODEL, D_MODEL), D_MODEL ** -0.5),
        "norm2_g": 1.0 + nrm(ks[17], (L, D_MODEL), 0.05),
        "w_ff1": nrm(ks[18], (L, D_MODEL, D_FF), D_MODEL ** -0.5),
        "w_ff2": nrm(ks[19], (L, D_FF, D_MODEL), D_FF ** -0.5),
        "norm_f_g": 1.0 + nrm(ks[20], (D_MODEL,), 0.05),
    }


def reference(x, c, w_ada, b_ada, norm1_g, w_in, ln_v_g, ln_v_b, w_spatial, b_spatial,
              w_pool, b_pool, pool_scale, b_gate, w_up_a, w_up_b, w_out, norm2_g,
              w_ff1, w_ff2, norm_f_g):
    c_act = jax.nn.silu(c)
    for l in range(DEPTH):
        mod = jnp.einsum('bd,de->be', c_act, w_ada[l]) + b_ada[l]
        shift1, scale1, gate1, shift2, scale2, gate2 = jnp.split(mod, N_MOD, axis=-1)
        h = modulate(rmsnorm(x, norm1_g[l]), shift1, scale1)
        y = token_mixer(h, w_in[l], ln_v_g[l], ln_v_b[l], w_spatial[l], b_spatial[l],
                        w_pool[l], b_pool[l], pool_scale[l], b_gate[l],
                        w_up_a[l], w_up_b[l], w_out[l])
        x = x + gate1[:, None, :] * y
        h = modulate(rmsnorm(x, norm2_g[l]), shift2, scale2)
        x = x + gate2[:, None, :] * channel_mixer(h, w_ff1[l], w_ff2[l])
    return rmsnorm(x, norm_f_g)
```

```python
import functools

import jax
import jax.numpy as jnp
from jax import lax
from jax.experimental import pallas as pl
from jax.experimental.pallas import tpu as pltpu

F32 = jnp.float32
BF16 = jnp.bfloat16

EPS = 1e-6
CHUNK = 128
HEAD_DIM = 128
POOL_WINDOWS = (2, 4, 8, 16)
HALO = 16
MIB = 1024 * 1024


def _params(sem, vmem_mib):
    return pltpu.CompilerParams(dimension_semantics=sem, vmem_limit_bytes=vmem_mib * MIB)


def _gelu_tanh(x):
    return 0.5 * x * (1.0 + jnp.tanh(0.7978845608028654 * (x + 0.044715 * (x * x * x))))


def _ada_kernel(c_ref, w_ref, b_ref, o_ref):
    c = c_ref[...]
    cact = c * jax.nn.sigmoid(c)
    o_ref[...] = jnp.sum(w_ref[...] * cact, axis=0, keepdims=True) + b_ref[...]


def _ada(c_col, w_ada, b_ada, tn=512):
    d, e = w_ada.shape
    return pl.pallas_call(
        _ada_kernel,
        grid=(e // tn,),
        in_specs=[pl.BlockSpec((d, 1), lambda j: (0, 0)),
                  pl.BlockSpec((d, tn), lambda j: (0, j)),
                  pl.BlockSpec((1, tn), lambda j: (0, j))],
        out_specs=pl.BlockSpec((1, tn), lambda j: (0, j)),
        out_shape=jax.ShapeDtypeStruct((1, e), F32),
        compiler_params=_params(("parallel",), 40),
        name="ada",
    )(c_col, w_ada, b_ada)


def _norm_mod_kernel(x_ref, g_ref, sc_ref, sh_ref, o_ref):
    x = x_ref[...]
    ms = jnp.mean(x * x, axis=-1, keepdims=True)
    w = g_ref[...] * (1.0 + sc_ref[...])
    o_ref[...] = (x * lax.rsqrt(ms + EPS) * w + sh_ref[...]).astype(o_ref.dtype)


def _norm_mod(x, g, scale, shift, tr=256):
    s, d = x.shape
    vec = pl.BlockSpec((1, d), lambda i: (0, 0))
    return pl.pallas_call(
        _norm_mod_kernel,
        grid=(s // tr,),
        in_specs=[pl.BlockSpec((tr, d), lambda i: (i, 0)), vec, vec, vec],
        out_specs=pl.BlockSpec((tr, d), lambda i: (i, 0)),
        out_shape=jax.ShapeDtypeStruct((s, d), BF16),
        compiler_params=_params(("parallel",), 40),
        name="norm_mod",
    )(x, g, scale, shift)


def _matmul_kernel(a_ref, w_ref, o_ref):
    o_ref[...] = jnp.dot(a_ref[...], w_ref[...], preferred_element_type=F32).astype(o_ref.dtype)


def _in_proj(h, w, tm=1024, tn=1024):
    s, k = h.shape
    _, n = w.shape
    return pl.pallas_call(
        _matmul_kernel,
        grid=(s // tm, n // tn),
        in_specs=[pl.BlockSpec((tm, k), lambda i, j: (i, 0)),
                  pl.BlockSpec((k, tn), lambda i, j: (0, j))],
        out_specs=pl.BlockSpec((tm, tn), lambda i, j: (i, j)),
        out_shape=jax.ShapeDtypeStruct((s, n), BF16),
        compiler_params=_params(("parallel", "parallel"), 48),
        name="in_proj",
    )(h, w)


def _mixer_kernel(u_ref, v_ref, p_ref, halo_ref, lng_ref, lnb_ref, ws_ref, bs_ref,
                  wp_ref, bp_ref, ps_ref, ya_ref, yb_ref,
                  wsm_ref, vn_ref, e_ref, pooled_ref, *, tm, d_a, pool_gd):
    i = pl.program_id(0)
    n_heads = d_a // HEAD_DIM
    rc = 32

    @pl.when(i == 0)
    def _():
        row = lax.broadcasted_iota(jnp.int32, (CHUNK, CHUNK), 0)
        col = lax.broadcasted_iota(jnp.int32, (CHUNK, CHUNK), 1)
        for h in range(n_heads):
            wsm_ref[h] = jnp.where(row >= col, ws_ref[h], 0.0).astype(BF16)

    def ln_body(r, carry):
        rows = pl.ds(pl.multiple_of(r * rc, rc), rc)
        v = _gelu_tanh(v_ref[rows, :].astype(F32))
        mu = jnp.mean(v, axis=-1, keepdims=True)
        vc = v - mu
        var = jnp.mean(vc * vc, axis=-1, keepdims=True)
        vn = vc * lax.rsqrt(var + EPS) * lng_ref[...] + lnb_ref[...]
        vn_ref[rows, :] = vn.astype(BF16)
        return carry
    lax.fori_loop(0, tm // rc, ln_body, 0)

    def sgu_body(c, carry):
        rows = pl.ds(pl.multiple_of(c * CHUNK, CHUNK), CHUNK)
        for h in range(n_heads):
            cols = slice(h * HEAD_DIM, (h + 1) * HEAD_DIM)
            mixed = jnp.dot(wsm_ref[h], vn_ref[rows, cols], preferred_element_type=F32)
            ug = _gelu_tanh(u_ref[rows, cols].astype(F32))
            ya_ref[rows, cols] = (ug * (mixed + bs_ref[:, cols])).astype(ya_ref.dtype)
        return carry
    lax.fori_loop(0, tm // CHUNK, sgu_body, 0)

    halo = halo_ref[...].astype(F32)
    e_ref[0:HALO, :] = jnp.where(i > 0, halo, 0.0)

    def copy_body(r, carry):
        rows = pl.multiple_of(r * rc, rc)
        e_ref[pl.ds(HALO + rows, rc), :] = p_ref[pl.ds(rows, rc), :].astype(F32)
        return carry
    lax.fori_loop(0, tm // rc, copy_body, 0)

    pc = 64
    for g, win in enumerate(POOL_WINDOWS):
        cols = slice(g * pool_gd, (g + 1) * pool_gd)
        for r0 in range(0, tm, pc):
            center = e_ref[HALO + r0:HALO + r0 + pc, cols]
            acc = center
            for k in range(1, win):
                acc = acc + e_ref[HALO + r0 - k:HALO + r0 - k + pc, cols]
            t = i * tm + r0 + lax.broadcasted_iota(jnp.int32, (pc, 1), 0)
            cnt = jnp.minimum(t + 1, win).astype(F32)
            pooled_ref[r0:r0 + pc, cols] = (acc / cnt - center).astype(BF16)
        y = jnp.dot(pooled_ref[:, cols], wp_ref[g], preferred_element_type=F32)
        yb_ref[:, cols] = ((y + bp_ref[:, cols]) * ps_ref[:, cols]).astype(yb_ref.dtype)


def _mixer(proj, ln_g, ln_b, w_spatial, bs_full, w_pool, b_pool, pool_scale, d_a, d_b, tm=256):
    s, _ = proj.shape
    n_heads = d_a // HEAD_DIM
    n_groups = len(POOL_WINDOWS)
    pool_gd = d_b // n_groups
    assert d_a == d_b and tm % CHUNK == 0 and tm % HALO == 0
    halo_blocks = tm // HALO
    vec_a = pl.BlockSpec((1, d_a), lambda i: (0, 0))
    vec_b = pl.BlockSpec((1, d_b), lambda i: (0, 0))
    kern = functools.partial(_mixer_kernel, tm=tm, d_a=d_a, pool_gd=pool_gd)
    return pl.pallas_call(
        kern,
        grid=(s // tm,),
        in_specs=[
            pl.BlockSpec((tm, d_a), lambda i: (i, 0)),
            pl.BlockSpec((tm, d_a), lambda i: (i, 1)),
            pl.BlockSpec((tm, d_b), lambda i: (i, 2)),
            pl.BlockSpec((HALO, d_b), lambda i: (jnp.maximum(i * halo_blocks - 1, 0), 2)),
            vec_a, vec_a,
            pl.BlockSpec((n_heads, CHUNK, CHUNK), lambda i: (0, 0, 0)),
            pl.BlockSpec((CHUNK, d_a), lambda i: (0, 0)),
            pl.BlockSpec((n_groups, pool_gd, pool_gd), lambda i: (0, 0, 0)),
            vec_b, vec_b,
        ],
        out_specs=[pl.BlockSpec((tm, d_a), lambda i: (i, 0)),
                   pl.BlockSpec((tm, d_b), lambda i: (i, 0))],
        out_shape=[jax.ShapeDtypeStruct((s, d_a), BF16),
                   jax.ShapeDtypeStruct((s, d_b), BF16)],
        scratch_shapes=[pltpu.VMEM((n_heads, CHUNK, CHUNK), BF16),
                        pltpu.VMEM((tm, d_a), BF16),
                        pltpu.VMEM((tm + HALO, d_b), F32),
                        pltpu.VMEM((tm, d_b), BF16)],
        compiler_params=_params(("arbitrary",), 40),
        name="mixer",
    )(proj, proj, proj, proj, ln_g, ln_b, w_spatial, bs_full, w_pool, b_pool, pool_scale)


def _merge_kernel(ya_ref, yb_ref, wa_ref, wb_ref, ga_ref, gb_ref, bga_ref, bgb_ref, o_ref):
    a = jnp.dot(ya_ref[...], wa_ref[...], preferred_element_type=F32)
    g_a = jax.nn.sigmoid(ga_ref[...].astype(F32) + bga_ref[...])
    m = g_a * a
    b = jnp.dot(yb_ref[...], wb_ref[...], preferred_element_type=F32)
    g_b = jax.nn.sigmoid(gb_ref[...].astype(F32) + bgb_ref[...])
    o_ref[...] = (m + g_b * b).astype(o_ref.dtype)


def _merge(ya, yb, wa, wb, proj, bga, bgb, d_model, tm=1024, tn=512):
    s, ka = ya.shape
    _, kb = yb.shape
    ga_off = (proj.shape[1] - 2 * d_model) // tn
    gb_off = (proj.shape[1] - d_model) // tn
    return pl.pallas_call(
        _merge_kernel,
        grid=(s // tm, d_model // tn),
        in_specs=[pl.BlockSpec((tm, ka), lambda i, j: (i, 0)),
                  pl.BlockSpec((tm, kb), lambda i, j: (i, 0)),
                  pl.BlockSpec((ka, tn), lambda i, j: (0, j)),
                  pl.BlockSpec((kb, tn), lambda i, j: (0, j)),
                  pl.BlockSpec((tm, tn), lambda i, j: (i, ga_off + j)),
                  pl.BlockSpec((tm, tn), lambda i, j: (i, gb_off + j)),
                  pl.BlockSpec((1, tn), lambda i, j: (0, j)),
                  pl.BlockSpec((1, tn), lambda i, j: (0, j))],
        out_specs=pl.BlockSpec((tm, tn), lambda i, j: (i, j)),
        out_shape=jax.ShapeDtypeStruct((s, d_model), BF16),
        compiler_params=_params(("parallel", "parallel"), 48),
        name="merge",
    )(ya, yb, wa, wb, proj, proj, bga, bgb)


def _out_proj_kernel(m_ref, w_ref, x_ref, gate_ref, o_ref):
    y = jnp.dot(m_ref[...], w_ref[...], preferred_element_type=F32)
    o_ref[...] = x_ref[...] + gate_ref[...] * y


def _out_proj(m, w, x, gate, tm=1024, tn=512):
    s, k = m.shape
    _, n = w.shape
    return pl.pallas_call(
        _out_proj_kernel,
        grid=(s // tm, n // tn),
        in_specs=[pl.BlockSpec((tm, k), lambda i, j: (i, 0)),
                  pl.BlockSpec((k, tn), lambda i, j: (0, j)),
                  pl.BlockSpec((tm, tn), lambda i, j: (i, j)),
                  pl.BlockSpec((1, tn), lambda i, j: (0, j))],
        out_specs=pl.BlockSpec((tm, tn), lambda i, j: (i, j)),
        out_shape=jax.ShapeDtypeStruct((s, n), F32),
        compiler_params=_params(("parallel", "parallel"), 52),
        name="out_proj",
    )(m, w, x, gate)


def _ffn_kernel(x_ref, g2_ref, sc_ref, sh_ref, gate_ref, gf_ref, w1_ref, w2_ref, o_ref, h_ref,
                *, tm, rc, nc):
    j = pl.program_id(1)

    @pl.when(j == 0)
    def _():
        w = g2_ref[...] * (1.0 + sc_ref[...])
        sh = sh_ref[...]

        def body(r, carry):
            rows = pl.ds(pl.multiple_of(r * rc, rc), rc)
            x = x_ref[rows, :]
            ms = jnp.mean(x * x, axis=-1, keepdims=True)
            h_ref[rows, :] = (x * lax.rsqrt(ms + EPS) * w + sh).astype(BF16)
            o_ref[rows, :] = jnp.zeros((rc, o_ref.shape[1]), F32)
            return carry
        lax.fori_loop(0, tm // rc, body, 0)

    a = jnp.dot(h_ref[...], w1_ref[...], preferred_element_type=F32)
    r = jnp.maximum(a, 0.0)
    r = (r * r).astype(BF16)
    for n0 in range(0, o_ref.shape[1], nc):
        o_ref[:, n0:n0 + nc] += jnp.dot(r, w2_ref[:, n0:n0 + nc], preferred_element_type=F32)

    @pl.when(j == pl.num_programs(1) - 1)
    def _():
        gate = gate_ref[...]
        gf = gf_ref[...]

        def body(r, carry):
            rows = pl.ds(pl.multiple_of(r * rc, rc), rc)
            x2 = x_ref[rows, :] + gate * o_ref[rows, :]
            ms = jnp.mean(x2 * x2, axis=-1, keepdims=True)
            o_ref[rows, :] = x2 * lax.rsqrt(ms + EPS) * gf
            return carry
        lax.fori_loop(0, tm // rc, body, 0)


def _ffn(x1, g2, scale, shift, gate, gf, w1, w2, tm=512, tf=512):
    s, d = x1.shape
    _, f = w1.shape
    vec = pl.BlockSpec((1, d), lambda i, j: (0, 0))
    kern = functools.partial(_ffn_kernel, tm=tm, rc=16, nc=1024)
    return pl.pallas_call(
        kern,
        grid=(s // tm, f // tf),
        in_specs=[pl.BlockSpec((tm, d), lambda i, j: (i, 0), pipeline_mode=pl.Buffered(1)),
                  vec, vec, vec, vec, vec,
                  pl.BlockSpec((d, tf), lambda i, j: (0, j)),
                  pl.BlockSpec((tf, d), lambda i, j: (j, 0))],
        out_specs=pl.BlockSpec((tm, d), lambda i, j: (i, 0)),
        out_shape=jax.ShapeDtypeStruct((s, d), F32),
        scratch_shapes=[pltpu.VMEM((tm, d), BF16)],
        compiler_params=_params(("parallel", "arbitrary"), 58),
        name="ffn",
    )(x1, g2, scale, shift, gate, gf, w1, w2)


def kernel(x, c, w_ada, b_ada, norm1_g, w_in, ln_v_g, ln_v_b, w_spatial, b_spatial, w_pool,
           b_pool, pool_scale, b_gate, w_up_a, w_up_b, w_out, norm2_g, w_ff1, w_ff2, norm_f_g):
    batch, seq, d_model = x.shape
    depth = w_ada.shape[0]
    d_a = ln_v_g.shape[-1]
    d_b = pool_scale.shape[-1]
    xs = x.reshape(batch * seq, d_model)
    assert batch == 1, "sequence-local mixers are tiled assuming one sequence"
    c_col = c.reshape(d_model, 1)

    for l in range(depth):
        mod = _ada(c_col, w_ada[l], b_ada[l].reshape(1, -1))
        shift1, scale1, gate1, shift2, scale2, gate2 = [
            mod[:, k * d_model:(k + 1) * d_model] for k in range(6)]

        h1 = _norm_mod(xs, norm1_g[l].reshape(1, -1), scale1, shift1)
        proj = _in_proj(h1, w_in[l].astype(BF16))

        bs_full = jnp.repeat(jnp.transpose(b_spatial[l]), HEAD_DIM, axis=1)
        ya, yb = _mixer(proj, ln_v_g[l].reshape(1, -1), ln_v_b[l].reshape(1, -1),
                        w_spatial[l], bs_full, w_pool[l].astype(BF16),
                        b_pool[l].reshape(1, -1), pool_scale[l].reshape(1, -1), d_a, d_b)

        merged = _merge(ya, yb, w_up_a[l].astype(BF16), w_up_b[l].astype(BF16), proj,
                        b_gate[l, 0].reshape(1, -1), b_gate[l, 1].reshape(1, -1), d_model)
        x1 = _out_proj(merged, w_out[l].astype(BF16), xs, gate1)

        assert depth == 1, "the final norm is fused into the last layer's channel mixer"
        xs = _ffn(x1, norm2_g[l].reshape(1, -1), scale2, shift2, gate2,
                  norm_f_g.reshape(1, -1), w_ff1[l].astype(BF16), w_ff2[l].astype(BF16))

    return xs.reshape(batch, seq, d_model)
```

```python
import functools

import jax
import jax.numpy as jnp
from jax import lax
from jax.experimental import pallas as pl
from jax.experimental.pallas import tpu as pltpu

F32 = jnp.float32
BF16 = jnp.bfloat16

EPS = 1e-6
CHUNK = 128
HEAD_DIM = 128
POOL_WINDOWS = (2, 4, 8, 16)
HALO = 16
MIB = 1024 * 1024


def _params(sem, vmem_mib):
    return pltpu.CompilerParams(dimension_semantics=sem, vmem_limit_bytes=vmem_mib * MIB)


def _gelu_tanh(x):
    return 0.5 * x * (1.0 + jnp.tanh(0.7978845608028654 * (x + 0.044715 * (x * x * x))))


def _ada_kernel(c_ref, w_ref, b_ref, o_ref):
    c = c_ref[...]
    cact = c * jax.nn.sigmoid(c)
    o_ref[...] = jnp.sum(w_ref[...] * cact, axis=0, keepdims=True) + b_ref[...]


def _ada(c_col, w_ada, b_ada, tn=512):
    d, e = w_ada.shape
    return pl.pallas_call(
        _ada_kernel,
        grid=(e // tn,),
        in_specs=[pl.BlockSpec((d, 1), lambda j: (0, 0)),
                  pl.BlockSpec((d, tn), lambda j: (0, j)),
                  pl.BlockSpec((1, tn), lambda j: (0, j))],
        out_specs=pl.BlockSpec((1, tn), lambda j: (0, j)),
        out_shape=jax.ShapeDtypeStruct((1, e), F32),
        compiler_params=_params(("parallel",), 40),
        name="ada",
    )(c_col, w_ada, b_ada)


def _norm_mod_kernel(x_ref, g_ref, sc_ref, sh_ref, o_ref):
    x = x_ref[...]
    ms = jnp.mean(x * x, axis=-1, keepdims=True)
    w = g_ref[...] * (1.0 + sc_ref[...])
    o_ref[...] = (x * lax.rsqrt(ms + EPS) * w + sh_ref[...]).astype(o_ref.dtype)


def _norm_mod(x, g, scale, shift, tr=256):
    s, d = x.shape
    vec = pl.BlockSpec((1, d), lambda i: (0, 0))
    return pl.pallas_call(
        _norm_mod_kernel,
        grid=(s // tr,),
        in_specs=[pl.BlockSpec((tr, d), lambda i: (i, 0)), vec, vec, vec],
        out_specs=pl.BlockSpec((tr, d), lambda i: (i, 0)),
        out_shape=jax.ShapeDtypeStruct((s, d), BF16),
        compiler_params=_params(("parallel",), 40),
        name="norm_mod",
    )(x, g, scale, shift)


def _in_proj_kernel(h_ref, w_ref, f1_ref, f2_ref, o_ref, f1o_ref, f2o_ref, *, nc):
    for n0 in range(0, o_ref.shape[1], nc):
        o_ref[:, n0:n0 + nc] = jnp.dot(
            h_ref[...], w_ref[:, n0:n0 + nc], preferred_element_type=F32).astype(o_ref.dtype)
    f1o_ref[...] = f1_ref[...].astype(BF16)
    f2o_ref[...] = f2_ref[...].astype(BF16)


def _in_proj(h, w, w_ff1, w_ff2, tm=1024, tn=1024, cast_blk=256):
    s, k = h.shape
    _, n = w.shape
    d, f = w_ff1.shape
    n_j = n // tn
    n_cast = f // cast_blk
    assert n_cast <= (s // tm) * n_j

    def cast_idx(i, j):
        return jnp.minimum(i * n_j + j, n_cast - 1)

    return pl.pallas_call(
        functools.partial(_in_proj_kernel, nc=512),
        grid=(s // tm, n_j),
        in_specs=[pl.BlockSpec((tm, k), lambda i, j: (i, 0), pipeline_mode=pl.Buffered(1)),
                  pl.BlockSpec((k, tn), lambda i, j: (0, j)),
                  pl.BlockSpec((d, cast_blk), lambda i, j: (0, cast_idx(i, j))),
                  pl.BlockSpec((cast_blk, d), lambda i, j: (cast_idx(i, j), 0))],
        out_specs=[pl.BlockSpec((tm, tn), lambda i, j: (i, j)),
                   pl.BlockSpec((d, cast_blk), lambda i, j: (0, cast_idx(i, j))),
                   pl.BlockSpec((cast_blk, d), lambda i, j: (cast_idx(i, j), 0))],
        out_shape=[jax.ShapeDtypeStruct((s, n), BF16),
                   jax.ShapeDtypeStruct((d, f), BF16),
                   jax.ShapeDtypeStruct((f, d), BF16)],
        compiler_params=_params(("arbitrary", "arbitrary"), 58),
        name="in_proj",
    )(h, w, w_ff1, w_ff2)


def _mixer_kernel(u_ref, v_ref, p_ref, halo_ref, lng_ref, lnb_ref, ws_ref, bs_ref,
                  wp_ref, bp_ref, ps_ref, wo_ref, ua_ref, ub_ref,
                  ya_ref, yb_ref, woo_ref, uao_ref, ubo_ref,
                  wsm_ref, band_ref, vn_ref, e_ref, pooled_ref, *, tm, d_a, pool_gd):
    i = pl.program_id(0)
    n_heads = d_a // HEAD_DIM
    rc = 32

    woo_ref[...] = wo_ref[...].astype(BF16)
    uao_ref[...] = ua_ref[...].astype(BF16)
    ubo_ref[...] = ub_ref[...].astype(BF16)

    @pl.when(i == 0)
    def _():
        row = lax.broadcasted_iota(jnp.int32, (CHUNK, CHUNK), 0)
        col = lax.broadcasted_iota(jnp.int32, (CHUNK, CHUNK), 1)
        for h in range(n_heads):
            wsm_ref[h] = jnp.where(row >= col, ws_ref[h], 0.0).astype(BF16)
        t = lax.broadcasted_iota(jnp.int32, (CHUNK, 2 * CHUNK), 0) + CHUNK
        s = lax.broadcasted_iota(jnp.int32, (CHUNK, 2 * CHUNK), 1)
        for g, win in enumerate(POOL_WINDOWS):
            band_ref[g] = jnp.where((s <= t) & (s > t - win), 1.0, 0.0).astype(BF16)
        e_ref[0:CHUNK - HALO, :] = jnp.zeros((CHUNK - HALO, e_ref.shape[1]), BF16)

    def ln_body(r, carry):
        rows = pl.ds(pl.multiple_of(r * rc, rc), rc)
        v = _gelu_tanh(v_ref[rows, :].astype(F32))
        mu = jnp.mean(v, axis=-1, keepdims=True)
        vc = v - mu
        var = jnp.mean(vc * vc, axis=-1, keepdims=True)
        vn = vc * lax.rsqrt(var + EPS) * lng_ref[...] + lnb_ref[...]
        vn_ref[rows, :] = vn.astype(BF16)
        return carry
    lax.fori_loop(0, tm // rc, ln_body, 0, unroll=2)

    def sgu_body(c, carry):
        rows = pl.ds(pl.multiple_of(c * CHUNK, CHUNK), CHUNK)
        for h in range(n_heads):
            cols = slice(h * HEAD_DIM, (h + 1) * HEAD_DIM)
            mixed = jnp.dot(wsm_ref[h], vn_ref[rows, cols], preferred_element_type=F32)
            ug = _gelu_tanh(u_ref[rows, cols].astype(F32))
            ya_ref[rows, cols] = (ug * (mixed + bs_ref[:, cols])).astype(ya_ref.dtype)
        return carry
    lax.fori_loop(0, tm // CHUNK, sgu_body, 0)

    e_ref[CHUNK - HALO:CHUNK, :] = jnp.where(i > 0, halo_ref[...], jnp.zeros_like(halo_ref))
    e_ref[CHUNK:, :] = p_ref[...]

    for g, win in enumerate(POOL_WINDOWS):
        cols = slice(g * pool_gd, (g + 1) * pool_gd)
        for r0 in range(0, tm, CHUNK):
            wsum = jnp.dot(band_ref[g], e_ref[r0:r0 + 2 * CHUNK, cols], preferred_element_type=F32)
            center = p_ref[r0:r0 + CHUNK, cols].astype(F32)
            t = i * tm + r0 + lax.broadcasted_iota(jnp.int32, (CHUNK, 1), 0)
            inv_cnt = 1.0 / jnp.minimum(t + 1, win).astype(F32)
            pooled_ref[r0:r0 + CHUNK, cols] = (wsum * inv_cnt - center).astype(BF16)
        y = jnp.dot(pooled_ref[:, cols], wp_ref[g], preferred_element_type=F32)
        yb_ref[:, cols] = ((y + bp_ref[:, cols]) * ps_ref[:, cols]).astype(yb_ref.dtype)


def _mixer(proj, ln_g, ln_b, w_spatial, bs_full, w_pool, b_pool, pool_scale,
           w_out, w_up_a, w_up_b, d_a, d_b, tm=256, cast_rows=128):
    s, _ = proj.shape
    d_model = w_out.shape[1]
    n_steps = s // tm
    n_heads = d_a // HEAD_DIM
    n_groups = len(POOL_WINDOWS)
    pool_gd = d_b // n_groups
    assert d_a == d_b and tm % CHUNK == 0 and tm % HALO == 0
    assert max(w_out.shape[0], w_up_a.shape[0], w_up_b.shape[0]) <= cast_rows * n_steps
    halo_blocks = tm // HALO
    vec_a = pl.BlockSpec((1, d_a), lambda i: (0, 0))
    vec_b = pl.BlockSpec((1, d_b), lambda i: (0, 0))

    def cast_spec(w):
        last = w.shape[0] // cast_rows - 1
        return pl.BlockSpec((cast_rows, w.shape[1]), lambda i: (jnp.minimum(i, last), 0))

    kern = functools.partial(_mixer_kernel, tm=tm, d_a=d_a, pool_gd=pool_gd)
    return pl.pallas_call(
        kern,
        grid=(n_steps,),
        in_specs=[
            pl.BlockSpec((tm, d_a), lambda i: (i, 0)),
            pl.BlockSpec((tm, d_a), lambda i: (i, 1)),
            pl.BlockSpec((tm, d_b), lambda i: (i, 2)),
            pl.BlockSpec((HALO, d_b), lambda i: (jnp.maximum(i * halo_blocks - 1, 0), 2)),
            vec_a, vec_a,
            pl.BlockSpec((n_heads, CHUNK, CHUNK), lambda i: (0, 0, 0)),
            pl.BlockSpec((CHUNK, d_a), lambda i: (0, 0)),
            pl.BlockSpec((n_groups, pool_gd, pool_gd), lambda i: (0, 0, 0)),
            vec_b, vec_b,
            cast_spec(w_out), cast_spec(w_up_a), cast_spec(w_up_b),
        ],
        out_specs=[pl.BlockSpec((tm, d_a), lambda i: (i, 0)),
                   pl.BlockSpec((tm, d_b), lambda i: (i, 0)),
                   cast_spec(w_out), cast_spec(w_up_a), cast_spec(w_up_b)],
        out_shape=[jax.ShapeDtypeStruct((s, d_a), BF16),
                   jax.ShapeDtypeStruct((s, d_b), BF16),
                   jax.ShapeDtypeStruct(w_out.shape, BF16),
                   jax.ShapeDtypeStruct(w_up_a.shape, BF16),
                   jax.ShapeDtypeStruct(w_up_b.shape, BF16)],
        scratch_shapes=[pltpu.VMEM((n_heads, CHUNK, CHUNK), BF16),
                        pltpu.VMEM((n_groups, CHUNK, 2 * CHUNK), BF16),
                        pltpu.VMEM((tm, d_a), BF16),
                        pltpu.VMEM((tm + CHUNK, d_b), BF16),
                        pltpu.VMEM((tm, d_b), BF16)],
        compiler_params=_params(("arbitrary",), 48),
        name="mixer",
    )(proj, proj, proj, proj, ln_g, ln_b, w_spatial, bs_full, w_pool, b_pool, pool_scale,
      w_out, w_up_a, w_up_b)


def _merge_kernel(ya_ref, yb_ref, wa_ref, wb_ref, ga_ref, gb_ref, bga_ref, bgb_ref, o_ref):
    a = jnp.dot(ya_ref[...], wa_ref[...], preferred_element_type=F32)
    g_a = jax.nn.sigmoid(ga_ref[...].astype(F32) + bga_ref[...])
    m = g_a * a
    b = jnp.dot(yb_ref[...], wb_ref[...], preferred_element_type=F32)
    g_b = jax.nn.sigmoid(gb_ref[...].astype(F32) + bgb_ref[...])
    o_ref[...] = (m + g_b * b).astype(o_ref.dtype)


def _merge(ya, yb, wa, wb, proj, bga, bgb, d_model, tm=1024, tn=512):
    s, ka = ya.shape
    _, kb = yb.shape
    ga_off = (proj.shape[1] - 2 * d_model) // tn
    gb_off = (proj.shape[1] - d_model) // tn
    return pl.pallas_call(
        _merge_kernel,
        grid=(s // tm, d_model // tn),
        in_specs=[pl.BlockSpec((tm, ka), lambda i, j: (i, 0)),
                  pl.BlockSpec((tm, kb), lambda i, j: (i, 0)),
                  pl.BlockSpec((ka, tn), lambda i, j: (0, j)),
                  pl.BlockSpec((kb, tn), lambda i, j: (0, j)),
                  pl.BlockSpec((tm, tn), lambda i, j: (i, ga_off + j)),
                  pl.BlockSpec((tm, tn), lambda i, j: (i, gb_off + j)),
                  pl.BlockSpec((1, tn), lambda i, j: (0, j)),
                  pl.BlockSpec((1, tn), lambda i, j: (0, j))],
        out_specs=pl.BlockSpec((tm, tn), lambda i, j: (i, j)),
        out_shape=jax.ShapeDtypeStruct((s, d_model), BF16),
        compiler_params=_params(("parallel", "parallel"), 48),
        name="merge",
    )(ya, yb, wa, wb, proj, proj, bga, bgb)


def _out_proj_kernel(m_ref, w_ref, x_ref, gate_ref, o_ref):
    y = jnp.dot(m_ref[...], w_ref[...], preferred_element_type=F32)
    o_ref[...] = x_ref[...] + gate_ref[...] * y


def _out_proj(m, w, x, gate, tm=1024, tn=512):
    s, k = m.shape
    _, n = w.shape
    return pl.pallas_call(
        _out_proj_kernel,
        grid=(s // tm, n // tn),
        in_specs=[pl.BlockSpec((tm, k), lambda i, j: (i, 0)),
                  pl.BlockSpec((k, tn), lambda i, j: (0, j)),
                  pl.BlockSpec((tm, tn), lambda i, j: (i, j)),
                  pl.BlockSpec((1, tn), lambda i, j: (0, j))],
        out_specs=pl.BlockSpec((tm, tn), lambda i, j: (i, j)),
        out_shape=jax.ShapeDtypeStruct((s, n), F32),
        compiler_params=_params(("parallel", "parallel"), 52),
        name="out_proj",
    )(m, w, x, gate)


def _ffn_kernel(x_ref, g2_ref, sc_ref, sh_ref, gate_ref, gf_ref, w1_ref, w2_ref, o_ref,
                h_ref, acc_ref, *, tm, rc, nc):
    j = pl.program_id(1)

    @pl.when(j == 0)
    def _():
        w = g2_ref[...] * (1.0 + sc_ref[...])
        sh = sh_ref[...]

        def body(r, carry):
            rows = pl.ds(pl.multiple_of(r * rc, rc), rc)
            x = x_ref[rows, :]
            ms = jnp.mean(x * x, axis=-1, keepdims=True)
            h_ref[rows, :] = (x * lax.rsqrt(ms + EPS) * w + sh).astype(BF16)
            acc_ref[rows, :] = jnp.zeros((rc, acc_ref.shape[1]), F32)
            return carry
        lax.fori_loop(0, tm // rc, body, 0, unroll=4)

    a = jnp.dot(h_ref[...], w1_ref[...], preferred_element_type=F32)
    r = jnp.maximum(a, 0.0)
    r = (r * r).astype(BF16)
    for n0 in range(0, acc_ref.shape[1], nc):
        acc_ref[:, n0:n0 + nc] += jnp.dot(r, w2_ref[:, n0:n0 + nc], preferred_element_type=F32)

    @pl.when(j == pl.num_programs(1) - 1)
    def _():
        gate = gate_ref[...]
        gf = gf_ref[...]

        def body(r, carry):
            rows = pl.ds(pl.multiple_of(r * rc, rc), rc)
            x2 = x_ref[rows, :] + gate * acc_ref[rows, :]
            ms = jnp.mean(x2 * x2, axis=-1, keepdims=True)
            o_ref[rows, :] = x2 * lax.rsqrt(ms + EPS) * gf
            return carry
        lax.fori_loop(0, tm // rc, body, 0, unroll=4)


def _ffn(x1, g2, scale, shift, gate, gf, w1, w2, tm=512, tf=512):
    s, d = x1.shape
    _, f = w1.shape
    vec = pl.BlockSpec((1, d), lambda i, j: (0, 0))
    kern = functools.partial(_ffn_kernel, tm=tm, rc=16, nc=1024)
    return pl.pallas_call(
        kern,
        grid=(s // tm, f // tf),
        in_specs=[pl.BlockSpec((tm, d), lambda i, j: (i, 0), pipeline_mode=pl.Buffered(1)),
                  vec, vec, vec, vec, vec,
                  pl.BlockSpec((d, tf), lambda i, j: (0, j)),
                  pl.BlockSpec((tf, d), lambda i, j: (j, 0))],
        out_specs=pl.BlockSpec((tm, d), lambda i, j: (i, 0)),
        out_shape=jax.ShapeDtypeStruct((s, d), F32),
        scratch_shapes=[pltpu.VMEM((tm, d), BF16), pltpu.VMEM((tm, d), F32)],
        compiler_params=_params(("parallel", "arbitrary"), 58),
        name="ffn",
    )(x1, g2, scale, shift, gate, gf, w1, w2)


def kernel(x, c, w_ada, b_ada, norm1_g, w_in, ln_v_g, ln_v_b, w_spatial, b_spatial, w_pool,
           b_pool, pool_scale, b_gate, w_up_a, w_up_b, w_out, norm2_g, w_ff1, w_ff2, norm_f_g):
    batch, seq, d_model = x.shape
    depth = w_ada.shape[0]
    d_a = ln_v_g.shape[-1]
    d_b = pool_scale.shape[-1]
    xs = x.reshape(batch * seq, d_model)
    assert batch == 1, "sequence-local mixers are tiled assuming one sequence"
    c_col = c.reshape(d_model, 1)

    for l in range(depth):
        mod = _ada(c_col, w_ada[l], b_ada[l].reshape(1, -1))
        shift1, scale1, gate1, shift2, scale2, gate2 = [
            mod[:, k * d_model:(k + 1) * d_model] for k in range(6)]

        h1 = _norm_mod(xs, norm1_g[l].reshape(1, -1), scale1, shift1)
        proj, w_ff1_bf, w_ff2_bf = _in_proj(h1, w_in[l].astype(BF16), w_ff1[l], w_ff2[l])

        bs_full = jnp.repeat(jnp.transpose(b_spatial[l]), HEAD_DIM, axis=1)
        ya, yb, w_out_bf, w_up_a_bf, w_up_b_bf = _mixer(
            proj, ln_v_g[l].reshape(1, -1), ln_v_b[l].reshape(1, -1), w_spatial[l], bs_full,
            w_pool[l].astype(BF16), b_pool[l].reshape(1, -1), pool_scale[l].reshape(1, -1),
            w_out[l], w_up_a[l], w_up_b[l], d_a, d_b)

        merged = _merge(ya, yb, w_up_a_bf, w_up_b_bf, proj,
                        b_gate[l, 0].reshape(1, -1), b_gate[l, 1].reshape(1, -1), d_model)
        x1 = _out_proj(merged, w_out_bf, xs, gate1)

        assert depth == 1, "the final norm is fused into the last layer's channel mixer"
        xs = _ffn(x1, norm2_g[l].reshape(1, -1), scale2, shift2, gate2,
                  norm_f_g.reshape(1, -1), w_ff1_bf, w_ff2_bf)

    return xs.reshape(batch, seq, d_model)
```

```python
import functools

import jax
import jax.numpy as jnp
from jax import lax
from jax.experimental import pallas as pl
from jax.experimental.pallas import tpu as pltpu

F32 = jnp.float32
BF16 = jnp.bfloat16

EPS = 1e-6
CHUNK = 128
HEAD_DIM = 128
POOL_WINDOWS = (2, 4, 8, 16)
HALO = 16
MIB = 1024 * 1024


def _params(sem, vmem_mib):
    return pltpu.CompilerParams(dimension_semantics=sem, vmem_limit_bytes=vmem_mib * MIB)


def _gelu_tanh(x):
    return 0.5 * x * (1.0 + jnp.tanh(0.7978845608028654 * (x + 0.044715 * (x * x * x))))


def _ada_kernel(c_ref, w_ref, b_ref, o_ref):
    c = c_ref[...]
    cact = c * jax.nn.sigmoid(c)
    o_ref[...] = jnp.sum(w_ref[...] * cact, axis=0, keepdims=True) + b_ref[...]


def _ada(c_col, w_ada, b_ada, tn=512):
    d, e = w_ada.shape
    return pl.pallas_call(
        _ada_kernel,
        grid=(e // tn,),
        in_specs=[pl.BlockSpec((d, 1), lambda j: (0, 0)),
                  pl.BlockSpec((d, tn), lambda j: (0, j)),
                  pl.BlockSpec((1, tn), lambda j: (0, j))],
        out_specs=pl.BlockSpec((1, tn), lambda j: (0, j)),
        out_shape=jax.ShapeDtypeStruct((1, e), F32),
        compiler_params=_params(("parallel",), 40),
        name="ada",
    )(c_col, w_ada, b_ada)


def _norm_mod_kernel(x_ref, g_ref, sc_ref, sh_ref, o_ref):
    x = x_ref[...]
    ms = jnp.mean(x * x, axis=-1, keepdims=True)
    w = g_ref[...] * (1.0 + sc_ref[...])
    o_ref[...] = (x * lax.rsqrt(ms + EPS) * w + sh_ref[...]).astype(o_ref.dtype)


def _norm_mod(x, g, scale, shift, tr=256):
    s, d = x.shape
    vec = pl.BlockSpec((1, d), lambda i: (0, 0))
    return pl.pallas_call(
        _norm_mod_kernel,
        grid=(s // tr,),
        in_specs=[pl.BlockSpec((tr, d), lambda i: (i, 0)), vec, vec, vec],
        out_specs=pl.BlockSpec((tr, d), lambda i: (i, 0)),
        out_shape=jax.ShapeDtypeStruct((s, d), BF16),
        compiler_params=_params(("parallel",), 40),
        name="norm_mod",
    )(x, g, scale, shift)


def _in_proj_kernel(h_ref, w_ref, f1_ref, f2_ref, o_ref, f1o_ref, f2o_ref, *, nc):
    f1o_ref[...] = f1_ref[...].astype(BF16)
    f2o_ref[...] = f2_ref[...].astype(BF16)
    for n0 in range(0, o_ref.shape[1], nc):
        o_ref[:, n0:n0 + nc] = jnp.dot(
            h_ref[...], w_ref[:, n0:n0 + nc], preferred_element_type=F32).astype(o_ref.dtype)


def _in_proj(h, w, w_ff1, w_ff2, tm=1024, tn=1024, cast_blk=256):
    s, k = h.shape
    _, n = w.shape
    d, f = w_ff1.shape
    n_j = n // tn
    n_cast = f // cast_blk
    assert n_cast <= (s // tm) * n_j

    def cast_idx(i, j):
        return jnp.minimum(i * n_j + j, n_cast - 1)

    return pl.pallas_call(
        functools.partial(_in_proj_kernel, nc=512),
        grid=(s // tm, n_j),
        in_specs=[pl.BlockSpec((tm, k), lambda i, j: (i, 0), pipeline_mode=pl.Buffered(1)),
                  pl.BlockSpec((k, tn), lambda i, j: (0, j)),
                  pl.BlockSpec((d, cast_blk), lambda i, j: (0, cast_idx(i, j))),
                  pl.BlockSpec((cast_blk, d), lambda i, j: (cast_idx(i, j), 0))],
        out_specs=[pl.BlockSpec((tm, tn), lambda i, j: (i, j)),
                   pl.BlockSpec((d, cast_blk), lambda i, j: (0, cast_idx(i, j))),
                   pl.BlockSpec((cast_blk, d), lambda i, j: (cast_idx(i, j), 0))],
        out_shape=[jax.ShapeDtypeStruct((s, n), BF16),
                   jax.ShapeDtypeStruct((d, f), BF16),
                   jax.ShapeDtypeStruct((f, d), BF16)],
        compiler_params=_params(("arbitrary", "arbitrary"), 58),
        name="in_proj",
    )(h, w, w_ff1, w_ff2)


def _mixer_kernel(u_ref, v_ref, p_ref, halo_ref, lng_ref, lnb_ref, ws_ref, bs_ref,
                  wp_ref, bp_ref, ps_ref, wo_ref, ua_ref, ub_ref,
                  ya_ref, yb_ref, woo_ref, uao_ref, ubo_ref,
                  wsm_ref, band_ref, vn_ref, e_ref, pooled_ref, *, tm, d_a, pool_gd):
    i = pl.program_id(0)
    n_heads = d_a // HEAD_DIM
    rc = 32

    woo_ref[...] = wo_ref[...].astype(BF16)
    uao_ref[...] = ua_ref[...].astype(BF16)
    ubo_ref[...] = ub_ref[...].astype(BF16)

    @pl.when(i == 0)
    def _():
        row = lax.broadcasted_iota(jnp.int32, (CHUNK, CHUNK), 0)
        col = lax.broadcasted_iota(jnp.int32, (CHUNK, CHUNK), 1)
        for h in range(n_heads):
            wsm_ref[h] = jnp.where(row >= col, ws_ref[h], 0.0).astype(BF16)
        t = lax.broadcasted_iota(jnp.int32, (CHUNK, 2 * CHUNK), 0) + CHUNK
        s = lax.broadcasted_iota(jnp.int32, (CHUNK, 2 * CHUNK), 1)
        for g, win in enumerate(POOL_WINDOWS):
            band_ref[g] = jnp.where((s <= t) & (s > t - win), 1.0, 0.0).astype(BF16)
        e_ref[0:CHUNK - HALO, :] = jnp.zeros((CHUNK - HALO, e_ref.shape[1]), BF16)

    def ln_body(r, carry):
        rows = pl.ds(pl.multiple_of(r * rc, rc), rc)
        v = _gelu_tanh(v_ref[rows, :].astype(F32))
        mu = jnp.mean(v, axis=-1, keepdims=True)
        vc = v - mu
        var = jnp.mean(vc * vc, axis=-1, keepdims=True)
        vn = vc * lax.rsqrt(var + EPS) * lng_ref[...] + lnb_ref[...]
        vn_ref[rows, :] = vn.astype(BF16)
        return carry
    lax.fori_loop(0, tm // rc, ln_body, 0, unroll=2)

    def sgu_body(c, carry):
        rows = pl.ds(pl.multiple_of(c * CHUNK, CHUNK), CHUNK)
        for h in range(n_heads):
            cols = slice(h * HEAD_DIM, (h + 1) * HEAD_DIM)
            mixed = jnp.dot(wsm_ref[h], vn_ref[rows, cols], preferred_element_type=F32)
            ug = _gelu_tanh(u_ref[rows, cols].astype(F32))
            ya_ref[rows, cols] = (ug * (mixed + bs_ref[:, cols])).astype(ya_ref.dtype)
        return carry
    lax.fori_loop(0, tm // CHUNK, sgu_body, 0)

    e_ref[CHUNK - HALO:CHUNK, :] = jnp.where(i > 0, halo_ref[...], jnp.zeros_like(halo_ref))
    e_ref[CHUNK:, :] = p_ref[...]

    for g, win in enumerate(POOL_WINDOWS):
        cols = slice(g * pool_gd, (g + 1) * pool_gd)
        for r0 in range(0, tm, CHUNK):
            wsum = jnp.dot(band_ref[g], e_ref[r0:r0 + 2 * CHUNK, cols], preferred_element_type=F32)
            center = p_ref[r0:r0 + CHUNK, cols].astype(F32)
            t = i * tm + r0 + lax.broadcasted_iota(jnp.int32, (CHUNK, 1), 0)
            inv_cnt = 1.0 / jnp.minimum(t + 1, win).astype(F32)
            pooled_ref[r0:r0 + CHUNK, cols] = (wsum * inv_cnt - center).astype(BF16)
        y = jnp.dot(pooled_ref[:, cols], wp_ref[g], preferred_element_type=F32)
        yb_ref[:, cols] = ((y + bp_ref[:, cols]) * ps_ref[:, cols]).astype(yb_ref.dtype)


def _mixer(proj, ln_g, ln_b, w_spatial, bs_full, w_pool, b_pool, pool_scale,
           w_out, w_up_a, w_up_b, d_a, d_b, tm=256, cast_rows=128):
    s, _ = proj.shape
    d_model = w_out.shape[1]
    n_steps = s // tm
    n_heads = d_a // HEAD_DIM
    n_groups = len(POOL_WINDOWS)
    pool_gd = d_b // n_groups
    assert d_a == d_b and tm % CHUNK == 0 and tm % HALO == 0
    assert max(w_out.shape[0], w_up_a.shape[0], w_up_b.shape[0]) <= cast_rows * n_steps
    halo_blocks = tm // HALO
    vec_a = pl.BlockSpec((1, d_a), lambda i: (0, 0))
    vec_b = pl.BlockSpec((1, d_b), lambda i: (0, 0))

    def cast_spec(w):
        last = w.shape[0] // cast_rows - 1
        return pl.BlockSpec((cast_rows, w.shape[1]), lambda i: (jnp.minimum(i, last), 0))

    kern = functools.partial(_mixer_kernel, tm=tm, d_a=d_a, pool_gd=pool_gd)
    return pl.pallas_call(
        kern,
        grid=(n_steps,),
        in_specs=[
            pl.BlockSpec((tm, d_a), lambda i: (i, 0)),
            pl.BlockSpec((tm, d_a), lambda i: (i, 1)),
            pl.BlockSpec((tm, d_b), lambda i: (i, 2)),
            pl.BlockSpec((HALO, d_b), lambda i: (jnp.maximum(i * halo_blocks - 1, 0), 2)),
            vec_a, vec_a,
            pl.BlockSpec((n_heads, CHUNK, CHUNK), lambda i: (0, 0, 0)),
            pl.BlockSpec((CHUNK, d_a), lambda i: (0, 0)),
            pl.BlockSpec((n_groups, pool_gd, pool_gd), lambda i: (0, 0, 0)),
            vec_b, vec_b,
            cast_spec(w_out), cast_spec(w_up_a), cast_spec(w_up_b),
        ],
        out_specs=[pl.BlockSpec((tm, d_a), lambda i: (i, 0)),
                   pl.BlockSpec((tm, d_b), lambda i: (i, 0)),
                   cast_spec(w_out), cast_spec(w_up_a), cast_spec(w_up_b)],
        out_shape=[jax.ShapeDtypeStruct((s, d_a), BF16),
                   jax.ShapeDtypeStruct((s, d_b), BF16),
                   jax.ShapeDtypeStruct(w_out.shape, BF16),
                   jax.ShapeDtypeStruct(w_up_a.shape, BF16),
                   jax.ShapeDtypeStruct(w_up_b.shape, BF16)],
        scratch_shapes=[pltpu.VMEM((n_heads, CHUNK, CHUNK), BF16),
                        pltpu.VMEM((n_groups, CHUNK, 2 * CHUNK), BF16),
                        pltpu.VMEM((tm, d_a), BF16),
                        pltpu.VMEM((tm + CHUNK, d_b), BF16),
                        pltpu.VMEM((tm, d_b), BF16)],
        compiler_params=_params(("arbitrary",), 48),
        name="mixer",
    )(proj, proj, proj, proj, ln_g, ln_b, w_spatial, bs_full, w_pool, b_pool, pool_scale,
      w_out, w_up_a, w_up_b)


def _merge_kernel(ya_ref, yb_ref, wa_ref, wb_ref, ga_ref, gb_ref, bga_ref, bgb_ref, o_ref):
    a = jnp.dot(ya_ref[...], wa_ref[...], preferred_element_type=F32)
    g_a = jax.nn.sigmoid(ga_ref[...].astype(F32) + bga_ref[...])
    m = g_a * a
    b = jnp.dot(yb_ref[...], wb_ref[...], preferred_element_type=F32)
    g_b = jax.nn.sigmoid(gb_ref[...].astype(F32) + bgb_ref[...])
    o_ref[...] = (m + g_b * b).astype(o_ref.dtype)


def _merge(ya, yb, wa, wb, proj, bga, bgb, d_model, tm=1024, tn=512):
    s, ka = ya.shape
    _, kb = yb.shape
    ga_off = (proj.shape[1] - 2 * d_model) // tn
    gb_off = (proj.shape[1] - d_model) // tn
    return pl.pallas_call(
        _merge_kernel,
        grid=(s // tm, d_model // tn),
        in_specs=[pl.BlockSpec((tm, ka), lambda i, j: (i, 0)),
                  pl.BlockSpec((tm, kb), lambda i, j: (i, 0)),
                  pl.BlockSpec((ka, tn), lambda i, j: (0, j)),
                  pl.BlockSpec((kb, tn), lambda i, j: (0, j)),
                  pl.BlockSpec((tm, tn), lambda i, j: (i, ga_off + j)),
                  pl.BlockSpec((tm, tn), lambda i, j: (i, gb_off + j)),
                  pl.BlockSpec((1, tn), lambda i, j: (0, j)),
                  pl.BlockSpec((1, tn), lambda i, j: (0, j))],
        out_specs=pl.BlockSpec((tm, tn), lambda i, j: (i, j)),
        out_shape=jax.ShapeDtypeStruct((s, d_model), BF16),
        compiler_params=_params(("parallel", "parallel"), 48),
        name="merge",
    )(ya, yb, wa, wb, proj, proj, bga, bgb)


def _out_proj_kernel(m_ref, w_ref, x_ref, gate_ref, o_ref):
    y = jnp.dot(m_ref[...], w_ref[...], preferred_element_type=F32)
    o_ref[...] = x_ref[...] + gate_ref[...] * y


def _out_proj(m, w, x, gate, tm=1024, tn=512):
    s, k = m.shape
    _, n = w.shape
    return pl.pallas_call(
        _out_proj_kernel,
        grid=(s // tm, n // tn),
        in_specs=[pl.BlockSpec((tm, k), lambda i, j: (i, 0)),
                  pl.BlockSpec((k, tn), lambda i, j: (0, j)),
                  pl.BlockSpec((tm, tn), lambda i, j: (i, j)),
                  pl.BlockSpec((1, tn), lambda i, j: (0, j))],
        out_specs=pl.BlockSpec((tm, tn), lambda i, j: (i, j)),
        out_shape=jax.ShapeDtypeStruct((s, n), F32),
        compiler_params=_params(("parallel", "parallel"), 52),
        name="out_proj",
    )(m, w, x, gate)


def _ffn_kernel(x_ref, g2_ref, sc_ref, sh_ref, gate_ref, gf_ref, w1_hbm, w2_hbm, o_hbm,
                h_ref, acc_ref, r_ref, w1buf, w2buf, obuf, sem1, sem2, semo,
                *, tm, tf, n_blk, n_slots, rc, nc, oc):
    i = pl.program_id(0)
    base = i * n_blk

    def w1_copy(g):
        blk = g % n_blk
        slot = g % n_slots
        return pltpu.make_async_copy(w1_hbm.at[:, pl.ds(blk * tf, tf)], w1buf.at[slot], sem1.at[slot])

    def w2_copy(g):
        blk = g % n_blk
        slot = g % n_slots
        return pltpu.make_async_copy(w2_hbm.at[pl.ds(blk * tf, tf), :], w2buf.at[slot], sem2.at[slot])

    def up(g, par):
        a = jnp.dot(h_ref[...], w1buf[g % n_slots], preferred_element_type=F32)
        r = jnp.maximum(a, 0.0)
        r_ref[par] = (r * r).astype(BF16)

    def down(g, par):
        slot = g % n_slots
        for n0 in range(0, acc_ref.shape[1], nc):
            acc_ref[:, n0:n0 + nc] += jnp.dot(
                r_ref[par], w2buf[slot, :, n0:n0 + nc], preferred_element_type=F32)

    @pl.when(i == 0)
    def _():
        for g in range(n_slots):
            w1_copy(g).start()
            w2_copy(g).start()

    w = g2_ref[...] * (1.0 + sc_ref[...])
    sh = sh_ref[...]

    def pro_body(r, carry):
        rows = pl.ds(pl.multiple_of(r * rc, rc), rc)
        x = x_ref[rows, :]
        ms = jnp.mean(x * x, axis=-1, keepdims=True)
        h_ref[rows, :] = (x * lax.rsqrt(ms + EPS) * w + sh).astype(BF16)
        acc_ref[rows, :] = jnp.zeros((rc, acc_ref.shape[1]), F32)
        return carry
    lax.fori_loop(0, tm // rc, pro_body, 0, unroll=4)

    w1_copy(base).wait()
    up(base, 0)
    w1_copy(base + n_slots).start()

    def blk_body(k, carry):
        g = base + k
        par = k % 2
        w2_copy(g).wait()
        w1_copy(g + 1).wait()
        down(g, par)
        up(g + 1, 1 - par)
        w2_copy(g + n_slots).start()
        w1_copy(g + 1 + n_slots).start()
        return carry
    lax.fori_loop(0, n_blk - 1, blk_body, 0)

    g_last = base + n_blk - 1
    w2_copy(g_last).wait()
    down(g_last, (n_blk - 1) % 2)
    w2_copy(g_last + n_slots).start()

    gate = gate_ref[...]
    gf = gf_ref[...]
    n_out = tm // oc

    def out_copy(c, slot):
        return pltpu.make_async_copy(
            obuf.at[slot], o_hbm.at[pl.ds(i * tm + c * oc, oc), :], semo.at[slot])

    def epi_body(c, carry):
        slot = c % 2

        @pl.when(c >= 2)
        def _():
            out_copy(c - 2, slot).wait()

        for r0 in range(0, oc, rc):
            rows = pl.ds(pl.multiple_of(c * oc + r0, rc), rc)
            x2 = x_ref[rows, :] + gate * acc_ref[rows, :]
            ms = jnp.mean(x2 * x2, axis=-1, keepdims=True)
            obuf[slot, r0:r0 + rc, :] = x2 * lax.rsqrt(ms + EPS) * gf
        out_copy(c, slot).start()
        return carry
    lax.fori_loop(0, n_out, epi_body, 0)
    out_copy(n_out - 2, n_out % 2).wait()
    out_copy(n_out - 1, (n_out - 1) % 2).wait()

    @pl.when(i == pl.num_programs(0) - 1)
    def _():
        end = base + n_blk
        for g in range(n_slots):
            w1_copy(end + g).wait()
            w2_copy(end + g).wait()


def _ffn(x1, g2, scale, shift, gate, gf, w1, w2, tm=512, tf=512, n_slots=3, oc=64):
    s, d = x1.shape
    _, f = w1.shape
    n_blk = f // tf
    assert n_blk % 2 == 0 and n_blk > n_slots and (tm // oc) >= 2
    vec = pl.BlockSpec((1, d), lambda i: (0, 0))
    hbm = pl.BlockSpec(memory_space=pl.ANY)
    kern = functools.partial(_ffn_kernel, tm=tm, tf=tf, n_blk=n_blk, n_slots=n_slots,
                             rc=16, nc=1024, oc=oc)
    return pl.pallas_call(
        kern,
        grid=(s // tm,),
        in_specs=[pl.BlockSpec((tm, d), lambda i: (i, 0), pipeline_mode=pl.Buffered(1)),
                  vec, vec, vec, vec, vec, hbm, hbm],
        out_specs=hbm,
        out_shape=jax.ShapeDtypeStruct((s, d), F32),
        scratch_shapes=[pltpu.VMEM((tm, d), BF16),
                        pltpu.VMEM((tm, d), F32),
                        pltpu.VMEM((2, tm, tf), BF16),
                        pltpu.VMEM((n_slots, d, tf), BF16),
                        pltpu.VMEM((n_slots, tf, d), BF16),
                        pltpu.VMEM((2, oc, d), F32),
                        pltpu.SemaphoreType.DMA((n_slots,)),
                        pltpu.SemaphoreType.DMA((n_slots,)),
                        pltpu.SemaphoreType.DMA((2,))],
        compiler_params=_params(("arbitrary",), 58),
        name="ffn",
    )(x1, g2, scale, shift, gate, gf, w1, w2)


def kernel(x, c, w_ada, b_ada, norm1_g, w_in, ln_v_g, ln_v_b, w_spatial, b_spatial, w_pool,
           b_pool, pool_scale, b_gate, w_up_a, w_up_b, w_out, norm2_g, w_ff1, w_ff2, norm_f_g):
    batch, seq, d_model = x.shape
    depth = w_ada.shape[0]
    d_a = ln_v_g.shape[-1]
    d_b = pool_scale.shape[-1]
    xs = x.reshape(batch * seq, d_model)
    assert batch == 1, "sequence-local mixers are tiled assuming one sequence"
    c_col = c.reshape(d_model, 1)

    for l in range(depth):
        mod = _ada(c_col, w_ada[l], b_ada[l].reshape(1, -1))
        shift1, scale1, gate1, shift2, scale2, gate2 = [
            mod[:, k * d_model:(k + 1) * d_model] for k in range(6)]

        h1 = _norm_mod(xs, norm1_g[l].reshape(1, -1), scale1, shift1)
        proj, w_ff1_bf, w_ff2_bf = _in_proj(h1, w_in[l].astype(BF16), w_ff1[l], w_ff2[l])

        bs_full = jnp.repeat(jnp.transpose(b_spatial[l]), HEAD_DIM, axis=1)
        ya, yb, w_out_bf, w_up_a_bf, w_up_b_bf = _mixer(
            proj, ln_v_g[l].reshape(1, -1), ln_v_b[l].reshape(1, -1), w_spatial[l], bs_full,
            w_pool[l].astype(BF16), b_pool[l].reshape(1, -1), pool_scale[l].reshape(1, -1),
            w_out[l], w_up_a[l], w_up_b[l], d_a, d_b)

        merged = _merge(ya, yb, w_up_a_bf, w_up_b_bf, proj,
                        b_gate[l, 0].reshape(1, -1), b_gate[l, 1].reshape(1, -1), d_model)
        x1 = _out_proj(merged, w_out_bf, xs, gate1)

        assert depth == 1, "the final norm is fused into the last layer's channel mixer"
        xs = _ffn(x1, norm2_g[l].reshape(1, -1), scale2, shift2, gate2,
                  norm_f_g.reshape(1, -1), w_ff1_bf, w_ff2_bf)

    return xs.reshape(batch, seq, d_model)
```

```python
import functools

import jax
import jax.numpy as jnp
from jax import lax
from jax.experimental import pallas as pl
from jax.experimental.pallas import tpu as pltpu

F32 = jnp.float32
BF16 = jnp.bfloat16

EPS = 1e-6
CHUNK = 128
HEAD_DIM = 128
POOL_WINDOWS = (2, 4, 8, 16)
HALO = 16
LANES = 128
MIB = 1024 * 1024


def _params(sem, vmem_mib):
    return pltpu.CompilerParams(dimension_semantics=sem, vmem_limit_bytes=vmem_mib * MIB)


def _gelu_tanh(x):
    return 0.5 * x * (1.0 + jnp.tanh(0.7978845608028654 * (x + 0.044715 * (x * x * x))))


def _matvec_cols(w_ref, cb_ref, b_ref, o_ref):
    cb = cb_ref[...]
    for n0 in range(0, o_ref.shape[1], LANES):
        cols = slice(n0, n0 + LANES)
        o_ref[:, cols] = jnp.sum(w_ref[:, cols] * cb, axis=0, keepdims=True) + b_ref[:, cols]


def _ada_kernel(c_ref, w_ref, b_ref, o_ref, cb_ref):
    @pl.when(pl.program_id(0) == 0)
    def _():
        c = c_ref[...]
        cb_ref[...] = jnp.broadcast_to(c * jax.nn.sigmoid(c), cb_ref.shape)
    _matvec_cols(w_ref, cb_ref, b_ref, o_ref)


def _ada(c_col, w_ada, b_ada, n_cols, tn=512):
    d, _ = w_ada.shape
    return pl.pallas_call(
        _ada_kernel,
        grid=(n_cols // tn,),
        in_specs=[pl.BlockSpec((d, 1), lambda j: (0, 0)),
                  pl.BlockSpec((d, tn), lambda j: (0, j)),
                  pl.BlockSpec((1, tn), lambda j: (0, j))],
        out_specs=[pl.BlockSpec((1, tn), lambda j: (0, j)),
                   pl.BlockSpec((d, LANES), lambda j: (0, 0))],
        out_shape=[jax.ShapeDtypeStruct((1, n_cols), F32),
                   jax.ShapeDtypeStruct((d, LANES), F32)],
        compiler_params=_params(("arbitrary",), 40),
        name="ada",
    )(c_col, w_ada, b_ada)


def _norm_mod_kernel(x_ref, g_ref, sc_ref, sh_ref, o_ref):
    x = x_ref[...]
    ms = jnp.mean(x * x, axis=-1, keepdims=True)
    w = g_ref[...] * (1.0 + sc_ref[...])
    o_ref[...] = (x * lax.rsqrt(ms + EPS) * w + sh_ref[...]).astype(o_ref.dtype)


def _norm_mod(x, g, scale, shift, tr=256):
    s, d = x.shape
    vec = pl.BlockSpec((1, d), lambda i: (0, 0))
    return pl.pallas_call(
        _norm_mod_kernel,
        grid=(s // tr,),
        in_specs=[pl.BlockSpec((tr, d), lambda i: (i, 0)), vec, vec, vec],
        out_specs=pl.BlockSpec((tr, d), lambda i: (i, 0)),
        out_shape=jax.ShapeDtypeStruct((s, d), BF16),
        compiler_params=_params(("parallel",), 40),
        name="norm_mod",
    )(x, g, scale, shift)


def _in_proj_kernel(h_ref, w_ref, f1_ref, f2_ref, o_ref, f1o_ref, f2o_ref, *, nc):
    f1o_ref[...] = f1_ref[...].astype(BF16)
    f2o_ref[...] = f2_ref[...].astype(BF16)
    for n0 in range(0, o_ref.shape[1], nc):
        o_ref[:, n0:n0 + nc] = jnp.dot(
            h_ref[...], w_ref[:, n0:n0 + nc], preferred_element_type=F32).astype(o_ref.dtype)


def _in_proj(h, w, w_ff1, w_ff2, tm=1024, tn=1024, cast_blk=256):
    s, k = h.shape
    _, n = w.shape
    d, f = w_ff1.shape
    n_j = n // tn
    n_cast = f // cast_blk
    assert n_cast <= (s // tm) * n_j

    def cast_idx(i, j):
        return jnp.minimum(i * n_j + j, n_cast - 1)

    return pl.pallas_call(
        functools.partial(_in_proj_kernel, nc=512),
        grid=(s // tm, n_j),
        in_specs=[pl.BlockSpec((tm, k), lambda i, j: (i, 0), pipeline_mode=pl.Buffered(1)),
                  pl.BlockSpec((k, tn), lambda i, j: (0, j)),
                  pl.BlockSpec((d, cast_blk), lambda i, j: (0, cast_idx(i, j))),
                  pl.BlockSpec((cast_blk, d), lambda i, j: (cast_idx(i, j), 0))],
        out_specs=[pl.BlockSpec((tm, tn), lambda i, j: (i, j)),
                   pl.BlockSpec((d, cast_blk), lambda i, j: (0, cast_idx(i, j))),
                   pl.BlockSpec((cast_blk, d), lambda i, j: (cast_idx(i, j), 0))],
        out_shape=[jax.ShapeDtypeStruct((s, n), BF16),
                   jax.ShapeDtypeStruct((d, f), BF16),
                   jax.ShapeDtypeStruct((f, d), BF16)],
        compiler_params=_params(("arbitrary", "arbitrary"), 58),
        name="in_proj",
    )(h, w, w_ff1, w_ff2)


def _mixer_kernel(u_ref, v_ref, p_ref, halo_ref, lng_ref, lnb_ref, ws_ref, bs_ref,
                  wp_ref, bp_ref, ps_ref, wo_ref, ua_ref, ub_ref,
                  ya_ref, yb_ref, woo_ref, uao_ref, ubo_ref,
                  wsm_ref, band_ref, vn_ref, e_ref, pooled_ref, *, tm, d_a, pool_gd):
    i = pl.program_id(0)
    n_heads = d_a // HEAD_DIM
    rc = 32

    woo_ref[...] = wo_ref[...].astype(BF16)
    uao_ref[...] = ua_ref[...].astype(BF16)
    ubo_ref[...] = ub_ref[...].astype(BF16)

    @pl.when(i == 0)
    def _():
        row = lax.broadcasted_iota(jnp.int32, (CHUNK, CHUNK), 0)
        col = lax.broadcasted_iota(jnp.int32, (CHUNK, CHUNK), 1)
        for h in range(n_heads):
            wsm_ref[h] = jnp.where(row >= col, ws_ref[h], 0.0).astype(BF16)
        t = lax.broadcasted_iota(jnp.int32, (CHUNK, 2 * CHUNK), 0) + CHUNK
        s = lax.broadcasted_iota(jnp.int32, (CHUNK, 2 * CHUNK), 1)
        for g, win in enumerate(POOL_WINDOWS):
            band_ref[g] = jnp.where((s <= t) & (s > t - win), 1.0, 0.0).astype(BF16)
        e_ref[0:CHUNK - HALO, :] = jnp.zeros((CHUNK - HALO, e_ref.shape[1]), BF16)

    def ln_body(r, carry):
        rows = pl.ds(pl.multiple_of(r * rc, rc), rc)
        v = _gelu_tanh(v_ref[rows, :].astype(F32))
        mu = jnp.mean(v, axis=-1, keepdims=True)
        vc = v - mu
        var = jnp.mean(vc * vc, axis=-1, keepdims=True)
        vn = vc * lax.rsqrt(var + EPS) * lng_ref[...] + lnb_ref[...]
        vn_ref[rows, :] = vn.astype(BF16)
        return carry
    lax.fori_loop(0, tm // rc, ln_body, 0, unroll=2)

    def sgu_body(c, carry):
        rows = pl.ds(pl.multiple_of(c * CHUNK, CHUNK), CHUNK)
        for h in range(n_heads):
            cols = slice(h * HEAD_DIM, (h + 1) * HEAD_DIM)
            mixed = jnp.dot(wsm_ref[h], vn_ref[rows, cols], preferred_element_type=F32)
            ug = _gelu_tanh(u_ref[rows, cols].astype(F32))
            ya_ref[rows, cols] = (ug * (mixed + bs_ref[:, cols])).astype(ya_ref.dtype)
        return carry
    lax.fori_loop(0, tm // CHUNK, sgu_body, 0)

    e_ref[CHUNK - HALO:CHUNK, :] = jnp.where(i > 0, halo_ref[...], jnp.zeros_like(halo_ref))
    e_ref[CHUNK:, :] = p_ref[...]

    for g, win in enumerate(POOL_WINDOWS):
        cols = slice(g * pool_gd, (g + 1) * pool_gd)
        for r0 in range(0, tm, CHUNK):
            wsum = jnp.dot(band_ref[g], e_ref[r0:r0 + 2 * CHUNK, cols], preferred_element_type=F32)
            center = p_ref[r0:r0 + CHUNK, cols].astype(F32)
            t = i * tm + r0 + lax.broadcasted_iota(jnp.int32, (CHUNK, 1), 0)
            inv_cnt = 1.0 / jnp.minimum(t + 1, win).astype(F32)
            pooled_ref[r0:r0 + CHUNK, cols] = (wsum * inv_cnt - center).astype(BF16)
        y = jnp.dot(pooled_ref[:, cols], wp_ref[g], preferred_element_type=F32)
        yb_ref[:, cols] = ((y + bp_ref[:, cols]) * ps_ref[:, cols]).astype(yb_ref.dtype)


def _mixer(proj, ln_g, ln_b, w_spatial, bs_full, w_pool, b_pool, pool_scale,
           w_out, w_up_a, w_up_b, d_a, d_b, tm=256, cast_rows=128):
    s, _ = proj.shape
    d_model = w_out.shape[1]
    n_steps = s // tm
    n_heads = d_a // HEAD_DIM
    n_groups = len(POOL_WINDOWS)
    pool_gd = d_b // n_groups
    assert d_a == d_b and tm % CHUNK == 0 and tm % HALO == 0
    assert max(w_out.shape[0], w_up_a.shape[0], w_up_b.shape[0]) <= cast_rows * n_steps
    halo_blocks = tm // HALO
    vec_a = pl.BlockSpec((1, d_a), lambda i: (0, 0))
    vec_b = pl.BlockSpec((1, d_b), lambda i: (0, 0))

    def cast_spec(w):
        last = w.shape[0] // cast_rows - 1
        return pl.BlockSpec((cast_rows, w.shape[1]), lambda i: (jnp.minimum(i, last), 0))

    kern = functools.partial(_mixer_kernel, tm=tm, d_a=d_a, pool_gd=pool_gd)
    return pl.pallas_call(
        kern,
        grid=(n_steps,),
        in_specs=[
            pl.BlockSpec((tm, d_a), lambda i: (i, 0)),
            pl.BlockSpec((tm, d_a), lambda i: (i, 1)),
            pl.BlockSpec((tm, d_b), lambda i: (i, 2)),
            pl.BlockSpec((HALO, d_b), lambda i: (jnp.maximum(i * halo_blocks - 1, 0), 2)),
            vec_a, vec_a,
            pl.BlockSpec((n_heads, CHUNK, CHUNK), lambda i: (0, 0, 0)),
            pl.BlockSpec((CHUNK, d_a), lambda i: (0, 0)),
            pl.BlockSpec((n_groups, pool_gd, pool_gd), lambda i: (0, 0, 0)),
            vec_b, vec_b,
            cast_spec(w_out), cast_spec(w_up_a), cast_spec(w_up_b),
        ],
        out_specs=[pl.BlockSpec((tm, d_a), lambda i: (i, 0)),
                   pl.BlockSpec((tm, d_b), lambda i: (i, 0)),
                   cast_spec(w_out), cast_spec(w_up_a), cast_spec(w_up_b)],
        out_shape=[jax.ShapeDtypeStruct((s, d_a), BF16),
                   jax.ShapeDtypeStruct((s, d_b), BF16),
                   jax.ShapeDtypeStruct(w_out.shape, BF16),
                   jax.ShapeDtypeStruct(w_up_a.shape, BF16),
                   jax.ShapeDtypeStruct(w_up_b.shape, BF16)],
        scratch_shapes=[pltpu.VMEM((n_heads, CHUNK, CHUNK), BF16),
                        pltpu.VMEM((n_groups, CHUNK, 2 * CHUNK), BF16),
                        pltpu.VMEM((tm, d_a), BF16),
                        pltpu.VMEM((tm + CHUNK, d_b), BF16),
                        pltpu.VMEM((tm, d_b), BF16)],
        compiler_params=_params(("arbitrary",), 48),
        name="mixer",
    )(proj, proj, proj, proj, ln_g, ln_b, w_spatial, bs_full, w_pool, b_pool, pool_scale,
      w_out, w_up_a, w_up_b)


def _merge_kernel(ya_ref, yb_ref, wa_ref, wb_ref, ga_ref, gb_ref, bga_ref, bgb_ref,
                  wada_ref, cb_ref, bada_ref, o_ref, mod_ref):
    _matvec_cols(wada_ref, cb_ref, bada_ref, mod_ref)
    a = jnp.dot(ya_ref[...], wa_ref[...], preferred_element_type=F32)
    g_a = jax.nn.sigmoid(ga_ref[...].astype(F32) + bga_ref[...])
    m = g_a * a
    b = jnp.dot(yb_ref[...], wb_ref[...], preferred_element_type=F32)
    g_b = jax.nn.sigmoid(gb_ref[...].astype(F32) + bgb_ref[...])
    o_ref[...] = (m + g_b * b).astype(o_ref.dtype)


def _merge(ya, yb, wa, wb, proj, bga, bgb, w_ada, cact_b, b_ada, ada_done, d_model,
           tm=1024, tn=512):
    s, ka = ya.shape
    _, kb = yb.shape
    d, e = w_ada.shape
    ga_off = (proj.shape[1] - 2 * d_model) // tn
    gb_off = (proj.shape[1] - d_model) // tn
    n_j = d_model // tn
    n_steps = (s // tm) * n_j
    ada_blk = (e - ada_done) // n_steps
    assert ada_blk * n_steps == e - ada_done and ada_blk % LANES == 0 and ada_done % ada_blk == 0
    ada_off = ada_done // ada_blk
    return pl.pallas_call(
        _merge_kernel,
        grid=(s // tm, n_j),
        in_specs=[pl.BlockSpec((tm, ka), lambda i, j: (i, 0)),
                  pl.BlockSpec((tm, kb), lambda i, j: (i, 0)),
                  pl.BlockSpec((ka, tn), lambda i, j: (0, j)),
                  pl.BlockSpec((kb, tn), lambda i, j: (0, j)),
                  pl.BlockSpec((tm, tn), lambda i, j: (i, ga_off + j)),
                  pl.BlockSpec((tm, tn), lambda i, j: (i, gb_off + j)),
                  pl.BlockSpec((1, tn), lambda i, j: (0, j)),
                  pl.BlockSpec((1, tn), lambda i, j: (0, j)),
                  pl.BlockSpec((d, ada_blk), lambda i, j: (0, ada_off + i * n_j + j)),
                  pl.BlockSpec((d, LANES), lambda i, j: (0, 0)),
                  pl.BlockSpec((1, ada_blk), lambda i, j: (0, ada_off + i * n_j + j))],
        out_specs=[pl.BlockSpec((tm, tn), lambda i, j: (i, j)),
                   pl.BlockSpec((1, ada_blk), lambda i, j: (0, i * n_j + j))],
        out_shape=[jax.ShapeDtypeStruct((s, d_model), BF16),
                   jax.ShapeDtypeStruct((1, e - ada_done), F32)],
        compiler_params=_params(("parallel", "parallel"), 56),
        name="merge",
    )(ya, yb, wa, wb, proj, proj, bga, bgb, w_ada, cact_b, b_ada)


def _out_proj_kernel(m_ref, w_ref, x_ref, gate_ref, o_ref):
    y = jnp.dot(m_ref[...], w_ref[...], preferred_element_type=F32)
    o_ref[...] = x_ref[...] + gate_ref[...] * y


def _out_proj(m, w, x, gate, tm=1024, tn=512):
    s, k = m.shape
    _, n = w.shape
    return pl.pallas_call(
        _out_proj_kernel,
        grid=(s // tm, n // tn),
        in_specs=[pl.BlockSpec((tm, k), lambda i, j: (i, 0)),
                  pl.BlockSpec((k, tn), lambda i, j: (0, j)),
                  pl.BlockSpec((tm, tn), lambda i, j: (i, j)),
                  pl.BlockSpec((1, tn), lambda i, j: (0, j))],
        out_specs=pl.BlockSpec((tm, tn), lambda i, j: (i, j)),
        out_shape=jax.ShapeDtypeStruct((s, n), F32),
        compiler_params=_params(("parallel", "parallel"), 52),
        name="out_proj",
    )(m, w, x, gate)


def _ffn_kernel(x_hbm, g2_ref, sc_ref, sh_ref, gate_ref, gf_ref, w1_hbm, w2_hbm, o_hbm,
                h_ref, acc_ref, r_ref, w1buf, w2buf, xbuf, obuf, sem1, sem2, semx, semo,
                *, tm, tf, n_blk, n_slots, n_xs, rc, nc, oc):
    i = pl.program_id(0)
    base = i * n_blk

    def w1_copy(g):
        blk = g % n_blk
        slot = g % n_slots
        return pltpu.make_async_copy(w1_hbm.at[:, pl.ds(blk * tf, tf)], w1buf.at[slot], sem1.at[slot])

    def w2_copy(g):
        blk = g % n_blk
        slot = g % n_slots
        return pltpu.make_async_copy(w2_hbm.at[pl.ds(blk * tf, tf), :], w2buf.at[slot], sem2.at[slot])

    def up(g, par):
        a = jnp.dot(h_ref[...], w1buf[g % n_slots], preferred_element_type=F32)
        r = jnp.maximum(a, 0.0)
        r_ref[par] = (r * r).astype(BF16)

    def down(g, par):
        slot = g % n_slots
        for n0 in range(0, acc_ref.shape[1], nc):
            acc_ref[:, n0:n0 + nc] += jnp.dot(
                r_ref[par], w2buf[slot, :, n0:n0 + nc], preferred_element_type=F32)

    @pl.when(i == 0)
    def _():
        for g in range(n_slots):
            w1_copy(g).start()
            w2_copy(g).start()

    n_out = tm // oc

    def x_copy(c):
        slot = c % n_xs
        return pltpu.make_async_copy(
            x_hbm.at[pl.ds(i * tm + c * oc, oc), :], xbuf.at[slot], semx.at[slot])

    def x_stream_begin():
        for c in range(n_xs - 1):
            x_copy(c).start()

    def x_stream_next(c):
        x_copy(c).wait()

        @pl.when(c + n_xs - 1 < n_out)
        def _():
            x_copy(c + n_xs - 1).start()

    w = g2_ref[...] * (1.0 + sc_ref[...])
    sh = sh_ref[...]
    x_stream_begin()

    def pro_body(c, carry):
        x_stream_next(c)
        slot = c % n_xs
        for r0 in range(0, oc, rc):
            rows = pl.ds(pl.multiple_of(c * oc + r0, rc), rc)
            x = xbuf[slot, r0:r0 + rc, :]
            ms = jnp.mean(x * x, axis=-1, keepdims=True)
            h_ref[rows, :] = (x * lax.rsqrt(ms + EPS) * w + sh).astype(BF16)
            acc_ref[rows, :] = jnp.zeros((rc, acc_ref.shape[1]), F32)
        return carry
    lax.fori_loop(0, n_out, pro_body, 0)

    w1_copy(base).wait()
    up(base, 0)
    w1_copy(base + n_slots).start()

    def blk_body(k, carry):
        g = base + k
        par = k % 2
        w2_copy(g).wait()
        w1_copy(g + 1).wait()
        down(g, par)
        up(g + 1, 1 - par)
        w2_copy(g + n_slots).start()
        w1_copy(g + 1 + n_slots).start()
        return carry
    lax.fori_loop(0, n_blk - 1, blk_body, 0)

    g_last = base + n_blk - 1
    x_stream_begin()
    w2_copy(g_last).wait()
    down(g_last, (n_blk - 1) % 2)
    w2_copy(g_last + n_slots).start()

    gate = gate_ref[...]
    gf = gf_ref[...]

    def out_copy(c, slot):
        return pltpu.make_async_copy(
            obuf.at[slot], o_hbm.at[pl.ds(i * tm + c * oc, oc), :], semo.at[slot])

    def epi_body(c, carry):
        slot = c % 2
        xslot = c % n_xs
        x_stream_next(c)

        @pl.when(c >= 2)
        def _():
            out_copy(c - 2, slot).wait()

        for r0 in range(0, oc, rc):
            rows = pl.ds(pl.multiple_of(c * oc + r0, rc), rc)
            x2 = xbuf[xslot, r0:r0 + rc, :] + gate * acc_ref[rows, :]
            ms = jnp.mean(x2 * x2, axis=-1, keepdims=True)
            obuf[slot, r0:r0 + rc, :] = x2 * lax.rsqrt(ms + EPS) * gf
        out_copy(c, slot).start()
        return carry
    lax.fori_loop(0, n_out, epi_body, 0)
    out_copy(n_out - 2, n_out % 2).wait()
    out_copy(n_out - 1, (n_out - 1) % 2).wait()

    @pl.when(i == pl.num_programs(0) - 1)
    def _():
        end = base + n_blk
        for g in range(n_slots):
            w1_copy(end + g).wait()
            w2_copy(end + g).wait()


def _ffn(x1, g2, scale, shift, gate, gf, w1, w2, tm=1024, tf=512, n_slots=2, n_xs=4, oc=64):
    s, d = x1.shape
    _, f = w1.shape
    n_blk = f // tf
    assert n_blk % 2 == 0 and n_blk > n_slots and (tm // oc) >= n_xs
    vec = pl.BlockSpec((1, d), lambda i: (0, 0))
    hbm = pl.BlockSpec(memory_space=pl.ANY)
    kern = functools.partial(_ffn_kernel, tm=tm, tf=tf, n_blk=n_blk, n_slots=n_slots,
                             n_xs=n_xs, rc=16, nc=1024, oc=oc)
    return pl.pallas_call(
        kern,
        grid=(s // tm,),
        in_specs=[hbm, vec, vec, vec, vec, vec, hbm, hbm],
        out_specs=hbm,
        out_shape=jax.ShapeDtypeStruct((s, d), F32),
        scratch_shapes=[pltpu.VMEM((tm, d), BF16),
                        pltpu.VMEM((tm, d), F32),
                        pltpu.VMEM((2, tm, tf), BF16),
                        pltpu.VMEM((n_slots, d, tf), BF16),
                        pltpu.VMEM((n_slots, tf, d), BF16),
                        pltpu.VMEM((n_xs, oc, d), F32),
                        pltpu.VMEM((2, oc, d), F32),
                        pltpu.SemaphoreType.DMA((n_slots,)),
                        pltpu.SemaphoreType.DMA((n_slots,)),
                        pltpu.SemaphoreType.DMA((n_xs,)),
                        pltpu.SemaphoreType.DMA((2,))],
        compiler_params=_params(("arbitrary",), 58),
        name="ffn",
    )(x1, g2, scale, shift, gate, gf, w1, w2)


def kernel(x, c, w_ada, b_ada, norm1_g, w_in, ln_v_g, ln_v_b, w_spatial, b_spatial, w_pool,
           b_pool, pool_scale, b_gate, w_up_a, w_up_b, w_out, norm2_g, w_ff1, w_ff2, norm_f_g):
    batch, seq, d_model = x.shape
    depth = w_ada.shape[0]
    d_a = ln_v_g.shape[-1]
    d_b = pool_scale.shape[-1]
    xs = x.reshape(batch * seq, d_model)
    assert batch == 1, "sequence-local mixers are tiled assuming one sequence"
    c_col = c.reshape(d_model, 1)

    for l in range(depth):
        b_ada_row = b_ada[l].reshape(1, -1)
        mod_a, cact_b = _ada(c_col, w_ada[l], b_ada_row, 2 * d_model)
        shift1, scale1 = mod_a[:, :d_model], mod_a[:, d_model:]

        h1 = _norm_mod(xs, norm1_g[l].reshape(1, -1), scale1, shift1)
        proj, w_ff1_bf, w_ff2_bf = _in_proj(h1, w_in[l].astype(BF16), w_ff1[l], w_ff2[l])

        bs_full = jnp.repeat(jnp.transpose(b_spatial[l]), HEAD_DIM, axis=1)
        ya, yb, w_out_bf, w_up_a_bf, w_up_b_bf = _mixer(
            proj, ln_v_g[l].reshape(1, -1), ln_v_b[l].reshape(1, -1), w_spatial[l], bs_full,
            w_pool[l].astype(BF16), b_pool[l].reshape(1, -1), pool_scale[l].reshape(1, -1),
            w_out[l], w_up_a[l], w_up_b[l], d_a, d_b)

        merged, mod_b = _merge(ya, yb, w_up_a_bf, w_up_b_bf, proj,
                               b_gate[l, 0].reshape(1, -1), b_gate[l, 1].reshape(1, -1),
                               w_ada[l], cact_b, b_ada_row, 2 * d_model, d_model)
        gate1, shift2, scale2, gate2 = [
            mod_b[:, k * d_model:(k + 1) * d_model] for k in range(4)]
        x1 = _out_proj(merged, w_out_bf, xs, gate1)

        assert depth == 1, "the final norm is fused into the last layer's channel mixer"
        xs = _ffn(x1, norm2_g[l].reshape(1, -1), scale2, shift2, gate2,
                  norm_f_g.reshape(1, -1), w_ff1_bf, w_ff2_bf)

    return xs.reshape(batch, seq, d_model)
```

```python
import functools

import jax
import jax.numpy as jnp
from jax import lax
from jax.experimental import pallas as pl
from jax.experimental.pallas import tpu as pltpu

F32 = jnp.float32
BF16 = jnp.bfloat16

EPS = 1e-6
CHUNK = 128
HEAD_DIM = 128
POOL_WINDOWS = (2, 4, 8, 16)
HALO = 16
LANES = 128
MIB = 1024 * 1024


def _params(sem, vmem_mib):
    return pltpu.CompilerParams(dimension_semantics=sem, vmem_limit_bytes=vmem_mib * MIB)


def _gelu_tanh(x):
    c = 0.7978845608028654
    half = 0.5 * x
    return half + half * jnp.tanh(x * (c + (c * 0.044715) * (x * x)))


def _matvec_cols(w_ref, cb_ref, b_ref, o_ref):
    cb = cb_ref[...]
    for n0 in range(0, o_ref.shape[1], LANES):
        cols = slice(n0, n0 + LANES)
        o_ref[:, cols] = jnp.sum(w_ref[:, cols] * cb, axis=0, keepdims=True) + b_ref[:, cols]


def _ada_kernel(c_ref, w_ref, b_ref, o_ref, cb_ref):
    @pl.when(pl.program_id(0) == 0)
    def _():
        c = c_ref[...]
        cb_ref[...] = jnp.broadcast_to(c * jax.nn.sigmoid(c), cb_ref.shape)
    _matvec_cols(w_ref, cb_ref, b_ref, o_ref)


def _ada(c_col, w_ada, b_ada, n_cols, tn=512):
    d, _ = w_ada.shape
    return pl.pallas_call(
        _ada_kernel,
        grid=(n_cols // tn,),
        in_specs=[pl.BlockSpec((d, 1), lambda j: (0, 0)),
                  pl.BlockSpec((d, tn), lambda j: (0, j)),
                  pl.BlockSpec((1, tn), lambda j: (0, j))],
        out_specs=[pl.BlockSpec((1, tn), lambda j: (0, j)),
                   pl.BlockSpec((d, LANES), lambda j: (0, 0))],
        out_shape=[jax.ShapeDtypeStruct((1, n_cols), F32),
                   jax.ShapeDtypeStruct((d, LANES), F32)],
        compiler_params=_params(("arbitrary",), 40),
        name="ada",
    )(c_col, w_ada, b_ada)


def _norm_mod_kernel(x_ref, g_ref, sc_ref, sh_ref, o_ref):
    x = x_ref[...]
    ms = jnp.mean(x * x, axis=-1, keepdims=True)
    w = g_ref[...] * (1.0 + sc_ref[...])
    o_ref[...] = (x * lax.rsqrt(ms + EPS) * w + sh_ref[...]).astype(o_ref.dtype)


def _norm_mod(x, g, scale, shift, tr=256):
    s, d = x.shape
    vec = pl.BlockSpec((1, d), lambda i: (0, 0))
    return pl.pallas_call(
        _norm_mod_kernel,
        grid=(s // tr,),
        in_specs=[pl.BlockSpec((tr, d), lambda i: (i, 0)), vec, vec, vec],
        out_specs=pl.BlockSpec((tr, d), lambda i: (i, 0)),
        out_shape=jax.ShapeDtypeStruct((s, d), BF16),
        compiler_params=_params(("parallel",), 40),
        name="norm_mod",
    )(x, g, scale, shift)


CAST_BLK = 128


def _cast_ff_block(f1_ref, f2_ref, f1o_ref, f2o_ref):
    f1o_ref[...] = f1_ref[...].astype(BF16)
    f2o_ref[...] = f2_ref[...].astype(BF16)


def _cast_ff_specs(d, first, n_here, step_of):
    def idx(*g):
        return first + jnp.minimum(step_of(*g), n_here - 1)
    return [pl.BlockSpec((d, CAST_BLK), lambda *g: (0, idx(*g))),
            pl.BlockSpec((CAST_BLK, d), lambda *g: (idx(*g), 0))]


def _in_proj_kernel(h_ref, w_ref, f1_ref, f2_ref, o_ref, f1o_ref, f2o_ref):
    _cast_ff_block(f1_ref, f2_ref, f1o_ref, f2o_ref)
    o_ref[...] = jnp.dot(h_ref[...], w_ref[...].astype(BF16),
                         preferred_element_type=F32).astype(o_ref.dtype)


def _in_proj(h, w, w_ff1, w_ff2, tm=1024, tn=512):
    s, k = h.shape
    _, n = w.shape
    d, f = w_ff1.shape
    n_j = n // tn
    n_steps = (s // tm) * n_j
    n_cast = f // CAST_BLK
    assert n_cast <= n_steps, "every block of w_ff1 / w_ff2 must get a grid step"
    cast_specs = _cast_ff_specs(d, 0, n_cast, lambda i, j: i * n_j + j)
    outs = pl.pallas_call(
        _in_proj_kernel,
        grid=(s // tm, n_j),
        in_specs=[pl.BlockSpec((tm, k), lambda i, j: (i, 0)),
                  pl.BlockSpec((k, tn), lambda i, j: (0, j))] + cast_specs,
        out_specs=[pl.BlockSpec((tm, tn), lambda i, j: (i, j))] + cast_specs,
        out_shape=[jax.ShapeDtypeStruct((s, n), BF16),
                   jax.ShapeDtypeStruct((d, f), BF16),
                   jax.ShapeDtypeStruct((f, d), BF16)],
        compiler_params=_params(("arbitrary", "arbitrary"), 56),
        name="in_proj",
    )(h, w, w_ff1, w_ff2)
    return outs


def _mixer_kernel(u_ref, v_ref, p_ref, halo_ref, lng_ref, lnb_ref, ws_ref, bs_ref,
                  wp_ref, bp_ref, ps_ref, wo_ref, ua_ref, ub_ref,
                  ya_ref, yb_ref, woo_ref, uao_ref, ubo_ref,
                  wsm_ref, band_ref, vn_ref, e_ref, pooled_ref, *, tm, d_a, pool_gd):
    i = pl.program_id(0)
    n_heads = d_a // HEAD_DIM
    rc = 32

    woo_ref[...] = wo_ref[...].astype(BF16)
    uao_ref[...] = ua_ref[...].astype(BF16)
    ubo_ref[...] = ub_ref[...].astype(BF16)

    @pl.when(i == 0)
    def _():
        row = lax.broadcasted_iota(jnp.int32, (CHUNK, CHUNK), 0)
        col = lax.broadcasted_iota(jnp.int32, (CHUNK, CHUNK), 1)
        for h in range(n_heads):
            wsm_ref[h] = jnp.where(row >= col, ws_ref[h], 0.0).astype(BF16)
        t = lax.broadcasted_iota(jnp.int32, (CHUNK, 2 * CHUNK), 0) + CHUNK
        s = lax.broadcasted_iota(jnp.int32, (CHUNK, 2 * CHUNK), 1)
        for g, win in enumerate(POOL_WINDOWS):
            band_ref[g] = jnp.where((s <= t) & (s > t - win), 1.0, 0.0).astype(BF16)
        e_ref[0:CHUNK - HALO, :] = jnp.zeros((CHUNK - HALO, e_ref.shape[1]), BF16)

    def ln_body(r, carry):
        rows = pl.ds(pl.multiple_of(r * rc, rc), rc)
        v = _gelu_tanh(v_ref[rows, :].astype(F32))
        mu = jnp.mean(v, axis=-1, keepdims=True)
        vc = v - mu
        var = jnp.mean(vc * vc, axis=-1, keepdims=True)
        vn = vc * lax.rsqrt(var + EPS) * lng_ref[...] + lnb_ref[...]
        vn_ref[rows, :] = vn.astype(BF16)
        return carry
    lax.fori_loop(0, tm // rc, ln_body, 0, unroll=2)

    def sgu_body(c, carry):
        rows = pl.ds(pl.multiple_of(c * CHUNK, CHUNK), CHUNK)
        for h in range(n_heads):
            cols = slice(h * HEAD_DIM, (h + 1) * HEAD_DIM)
            mixed = jnp.dot(wsm_ref[h], vn_ref[rows, cols], preferred_element_type=F32)
            ug = _gelu_tanh(u_ref[rows, cols].astype(F32))
            ya_ref[rows, cols] = (ug * (mixed + bs_ref[:, cols])).astype(ya_ref.dtype)
        return carry
    lax.fori_loop(0, tm // CHUNK, sgu_body, 0)

    e_ref[CHUNK - HALO:CHUNK, :] = jnp.where(i > 0, halo_ref[...], jnp.zeros_like(halo_ref))
    e_ref[CHUNK:, :] = p_ref[...]

    for g, win in enumerate(POOL_WINDOWS):
        cols = slice(g * pool_gd, (g + 1) * pool_gd)
        for r0 in range(0, tm, CHUNK):
            wsum = jnp.dot(band_ref[g], e_ref[r0:r0 + 2 * CHUNK, cols], preferred_element_type=F32)
            center = p_ref[r0:r0 + CHUNK, cols].astype(F32)
            t = i * tm + r0 + lax.broadcasted_iota(jnp.int32, (CHUNK, 1), 0)
            inv_cnt = 1.0 / jnp.minimum(t + 1, win).astype(F32)
            pooled_ref[r0:r0 + CHUNK, cols] = (wsum * inv_cnt - center).astype(BF16)
        y = jnp.dot(pooled_ref[:, cols], wp_ref[g], preferred_element_type=F32)
        yb_ref[:, cols] = ((y + bp_ref[:, cols]) * ps_ref[:, cols]).astype(yb_ref.dtype)


def _mixer(proj, ln_g, ln_b, w_spatial, bs_full, w_pool, b_pool, pool_scale,
           w_out, w_up_a, w_up_b, d_a, d_b, tm=256, cast_rows=128):
    s, _ = proj.shape
    d_model = w_out.shape[1]
    n_steps = s // tm
    n_heads = d_a // HEAD_DIM
    n_groups = len(POOL_WINDOWS)
    pool_gd = d_b // n_groups
    assert d_a == d_b and tm % CHUNK == 0 and tm % HALO == 0
    assert max(w_out.shape[0], w_up_a.shape[0], w_up_b.shape[0]) <= cast_rows * n_steps
    halo_blocks = tm // HALO
    vec_a = pl.BlockSpec((1, d_a), lambda i: (0, 0))
    vec_b = pl.BlockSpec((1, d_b), lambda i: (0, 0))

    def cast_spec(w):
        last = w.shape[0] // cast_rows - 1
        return pl.BlockSpec((cast_rows, w.shape[1]), lambda i: (jnp.minimum(i, last), 0))

    kern = functools.partial(_mixer_kernel, tm=tm, d_a=d_a, pool_gd=pool_gd)
    return pl.pallas_call(
        kern,
        grid=(n_steps,),
        in_specs=[
            pl.BlockSpec((tm, d_a), lambda i: (i, 0)),
            pl.BlockSpec((tm, d_a), lambda i: (i, 1)),
            pl.BlockSpec((tm, d_b), lambda i: (i, 2)),
            pl.BlockSpec((HALO, d_b), lambda i: (jnp.maximum(i * halo_blocks - 1, 0), 2)),
            vec_a, vec_a,
            pl.BlockSpec((n_heads, CHUNK, CHUNK), lambda i: (0, 0, 0)),
            pl.BlockSpec((CHUNK, d_a), lambda i: (0, 0)),
            pl.BlockSpec((n_groups, pool_gd, pool_gd), lambda i: (0, 0, 0)),
            vec_b, vec_b,
            cast_spec(w_out), cast_spec(w_up_a), cast_spec(w_up_b),
        ],
        out_specs=[pl.BlockSpec((tm, d_a), lambda i: (i, 0)),
                   pl.BlockSpec((tm, d_b), lambda i: (i, 0)),
                   cast_spec(w_out), cast_spec(w_up_a), cast_spec(w_up_b)],
        out_shape=[jax.ShapeDtypeStruct((s, d_a), BF16),
                   jax.ShapeDtypeStruct((s, d_b), BF16),
                   jax.ShapeDtypeStruct(w_out.shape, BF16),
                   jax.ShapeDtypeStruct(w_up_a.shape, BF16),
                   jax.ShapeDtypeStruct(w_up_b.shape, BF16)],
        scratch_shapes=[pltpu.VMEM((n_heads, CHUNK, CHUNK), BF16),
                        pltpu.VMEM((n_groups, CHUNK, 2 * CHUNK), BF16),
                        pltpu.VMEM((tm, d_a), BF16),
                        pltpu.VMEM((tm + CHUNK, d_b), BF16),
                        pltpu.VMEM((tm, d_b), BF16)],
        compiler_params=_params(("arbitrary",), 48),
        name="mixer",
    )(proj, proj, proj, proj, ln_g, ln_b, w_spatial, bs_full, w_pool, b_pool, pool_scale,
      w_out, w_up_a, w_up_b)


def _merge_kernel(ya_ref, yb_ref, wa_ref, wb_ref, ga_ref, gb_ref, bga_ref, bgb_ref,
                  wada_ref, cb_ref, bada_ref, o_ref, mod_ref):
    _matvec_cols(wada_ref, cb_ref, bada_ref, mod_ref)
    a = jnp.dot(ya_ref[...], wa_ref[...], preferred_element_type=F32)
    g_a = jax.nn.sigmoid(ga_ref[...].astype(F32) + bga_ref[...])
    m = g_a * a
    b = jnp.dot(yb_ref[...], wb_ref[...], preferred_element_type=F32)
    g_b = jax.nn.sigmoid(gb_ref[...].astype(F32) + bgb_ref[...])
    o_ref[...] = (m + g_b * b).astype(o_ref.dtype)


def _merge(ya, yb, wa, wb, proj, bga, bgb, w_ada, cact_b, b_ada, ada_done, d_model,
           tm=1024, tn=512):
    s, ka = ya.shape
    _, kb = yb.shape
    d, e = w_ada.shape
    ga_off = (proj.shape[1] - 2 * d_model) // tn
    gb_off = (proj.shape[1] - d_model) // tn
    n_j = d_model // tn
    n_steps = (s // tm) * n_j
    ada_blk = (e - ada_done) // n_steps
    assert ada_blk * n_steps == e - ada_done and ada_blk % LANES == 0 and ada_done % ada_blk == 0
    ada_off = ada_done // ada_blk
    return pl.pallas_call(
        _merge_kernel,
        grid=(s // tm, n_j),
        in_specs=[pl.BlockSpec((tm, ka), lambda i, j: (i, 0)),
                  pl.BlockSpec((tm, kb), lambda i, j: (i, 0)),
                  pl.BlockSpec((ka, tn), lambda i, j: (0, j)),
                  pl.BlockSpec((kb, tn), lambda i, j: (0, j)),
                  pl.BlockSpec((tm, tn), lambda i, j: (i, ga_off + j)),
                  pl.BlockSpec((tm, tn), lambda i, j: (i, gb_off + j)),
                  pl.BlockSpec((1, tn), lambda i, j: (0, j)),
                  pl.BlockSpec((1, tn), lambda i, j: (0, j)),
                  pl.BlockSpec((d, ada_blk), lambda i, j: (0, ada_off + i * n_j + j)),
                  pl.BlockSpec((d, LANES), lambda i, j: (0, 0)),
                  pl.BlockSpec((1, ada_blk), lambda i, j: (0, ada_off + i * n_j + j))],
        out_specs=[pl.BlockSpec((tm, tn), lambda i, j: (i, j)),
                   pl.BlockSpec((1, ada_blk), lambda i, j: (0, i * n_j + j))],
        out_shape=[jax.ShapeDtypeStruct((s, d_model), BF16),
                   jax.ShapeDtypeStruct((1, e - ada_done), F32)],
        compiler_params=_params(("parallel", "parallel"), 56),
        name="merge",
    )(ya, yb, wa, wb, proj, proj, bga, bgb, w_ada, cact_b, b_ada)


def _out_proj_kernel(m_ref, w_ref, x_ref, gate_ref, o_ref):
    y = jnp.dot(m_ref[...], w_ref[...], preferred_element_type=F32)
    o_ref[...] = x_ref[...] + gate_ref[...] * y


def _out_proj(m, w, x, gate, tm=1024, tn=512):
    s, k = m.shape
    _, n = w.shape
    return pl.pallas_call(
        _out_proj_kernel,
        grid=(s // tm, n // tn),
        in_specs=[pl.BlockSpec((tm, k), lambda i, j: (i, 0)),
                  pl.BlockSpec((k, tn), lambda i, j: (0, j)),
                  pl.BlockSpec((tm, tn), lambda i, j: (i, j)),
                  pl.BlockSpec((1, tn), lambda i, j: (0, j))],
        out_specs=pl.BlockSpec((tm, tn), lambda i, j: (i, j)),
        out_shape=jax.ShapeDtypeStruct((s, n), F32),
        compiler_params=_params(("parallel", "parallel"), 52),
        name="out_proj",
    )(m, w, x, gate)


def _ffn_kernel(x_hbm, g2_ref, sc_ref, sh_ref, gate_ref, gf_ref, w1_hbm, w2_hbm, o_hbm,
                h_ref, acc_ref, r_ref, w1buf, w2buf, xbuf, obuf, sem1, sem2, semx, semo,
                *, tm, tf, n_blk, n_slots, n_xs, rc, nc, oc):
    i = pl.program_id(0)
    base = i * n_blk

    def w1_copy(g):
        blk = g % n_blk
        slot = g % n_slots
        return pltpu.make_async_copy(w1_hbm.at[:, pl.ds(blk * tf, tf)], w1buf.at[slot], sem1.at[slot])

    def w2_copy(g):
        blk = g % n_blk
        slot = g % n_slots
        return pltpu.make_async_copy(w2_hbm.at[pl.ds(blk * tf, tf), :], w2buf.at[slot], sem2.at[slot])

    def up(g, par):
        a = jnp.dot(h_ref[...], w1buf[g % n_slots], preferred_element_type=F32)
        r = jnp.maximum(a, 0.0)
        r_ref[par] = (r * r).astype(BF16)

    def down(g, par):
        slot = g % n_slots
        for n0 in range(0, acc_ref.shape[1], nc):
            acc_ref[:, n0:n0 + nc] += jnp.dot(
                r_ref[par], w2buf[slot, :, n0:n0 + nc], preferred_element_type=F32)

    @pl.when(i == 0)
    def _():
        for g in range(n_slots):
            w1_copy(g).start()
            w2_copy(g).start()

    n_out = tm // oc

    def x_copy(c):
        slot = c % n_xs
        return pltpu.make_async_copy(
            x_hbm.at[pl.ds(i * tm + c * oc, oc), :], xbuf.at[slot], semx.at[slot])

    def x_stream_begin():
        for c in range(n_xs - 1):
            x_copy(c).start()

    def x_stream_next(c):
        x_copy(c).wait()

        @pl.when(c + n_xs - 1 < n_out)
        def _():
            x_copy(c + n_xs - 1).start()

    w = g2_ref[...] * (1.0 + sc_ref[...])
    sh = sh_ref[...]
    x_stream_begin()

    def pro_body(c, carry):
        x_stream_next(c)
        slot = c % n_xs
        for r0 in range(0, oc, rc):
            rows = pl.ds(pl.multiple_of(c * oc + r0, rc), rc)
            x = xbuf[slot, r0:r0 + rc, :]
            ms = jnp.mean(x * x, axis=-1, keepdims=True)
            h_ref[rows, :] = (x * lax.rsqrt(ms + EPS) * w + sh).astype(BF16)
            acc_ref[rows, :] = jnp.zeros((rc, acc_ref.shape[1]), F32)
        return carry
    lax.fori_loop(0, n_out, pro_body, 0)

    w1_copy(base).wait()
    up(base, 0)
    w1_copy(base + n_slots).start()

    def blk_body(k, carry):
        g = base + k
        par = k % 2
        w2_copy(g).wait()
        w1_copy(g + 1).wait()
        down(g, par)
        up(g + 1, 1 - par)
        w2_copy(g + n_slots).start()
        w1_copy(g + 1 + n_slots).start()
        return carry
    lax.fori_loop(0, n_blk - 1, blk_body, 0)

    g_last = base + n_blk - 1
    x_stream_begin()
    w2_copy(g_last).wait()
    down(g_last, (n_blk - 1) % 2)
    w2_copy(g_last + n_slots).start()

    gate = gate_ref[...]
    gf = gf_ref[...]

    def out_copy(c, slot):
        return pltpu.make_async_copy(
            obuf.at[slot], o_hbm.at[pl.ds(i * tm + c * oc, oc), :], semo.at[slot])

    def epi_body(c, carry):
        slot = c % 2
        xslot = c % n_xs
        x_stream_next(c)

        @pl.when(c >= 2)
        def _():
            out_copy(c - 2, slot).wait()

        for r0 in range(0, oc, rc):
            rows = pl.ds(pl.multiple_of(c * oc + r0, rc), rc)
            x2 = xbuf[xslot, r0:r0 + rc, :] + gate * acc_ref[rows, :]
            ms = jnp.mean(x2 * x2, axis=-1, keepdims=True)
            obuf[slot, r0:r0 + rc, :] = x2 * lax.rsqrt(ms + EPS) * gf
        out_copy(c, slot).start()
        return carry
    lax.fori_loop(0, n_out, epi_body, 0)
    out_copy(n_out - 2, n_out % 2).wait()
    out_copy(n_out - 1, (n_out - 1) % 2).wait()

    @pl.when(i == pl.num_programs(0) - 1)
    def _():
        end = base + n_blk
        for g in range(n_slots):
            w1_copy(end + g).wait()
            w2_copy(end + g).wait()


def _ffn(x1, g2, scale, shift, gate, gf, w1, w2, tm=1024, tf=512, n_slots=2, n_xs=3, oc=128):
    s, d = x1.shape
    _, f = w1.shape
    n_blk = f // tf
    assert n_blk % 2 == 0 and n_blk > n_slots and (tm // oc) >= n_xs
    vec = pl.BlockSpec((1, d), lambda i: (0, 0))
    hbm = pl.BlockSpec(memory_space=pl.ANY)
    kern = functools.partial(_ffn_kernel, tm=tm, tf=tf, n_blk=n_blk, n_slots=n_slots,
                             n_xs=n_xs, rc=16, nc=1024, oc=oc)
    return pl.pallas_call(
        kern,
        grid=(s // tm,),
        in_specs=[hbm, vec, vec, vec, vec, vec, hbm, hbm],
        out_specs=hbm,
        out_shape=jax.ShapeDtypeStruct((s, d), F32),
        scratch_shapes=[pltpu.VMEM((tm, d), BF16),
                        pltpu.VMEM((tm, d), F32),
                        pltpu.VMEM((2, tm, tf), BF16),
                        pltpu.VMEM((n_slots, d, tf), BF16),
                        pltpu.VMEM((n_slots, tf, d), BF16),
                        pltpu.VMEM((n_xs, oc, d), F32),
                        pltpu.VMEM((2, oc, d), F32),
                        pltpu.SemaphoreType.DMA((n_slots,)),
                        pltpu.SemaphoreType.DMA((n_slots,)),
                        pltpu.SemaphoreType.DMA((n_xs,)),
                        pltpu.SemaphoreType.DMA((2,))],
        compiler_params=_params(("arbitrary",), 58),
        name="ffn",
    )(x1, g2, scale, shift, gate, gf, w1, w2)


def kernel(x, c, w_ada, b_ada, norm1_g, w_in, ln_v_g, ln_v_b, w_spatial, b_spatial, w_pool,
           b_pool, pool_scale, b_gate, w_up_a, w_up_b, w_out, norm2_g, w_ff1, w_ff2, norm_f_g):
    batch, seq, d_model = x.shape
    depth = w_ada.shape[0]
    d_a = ln_v_g.shape[-1]
    d_b = pool_scale.shape[-1]
    xs = x.reshape(batch * seq, d_model)
    assert batch == 1, "sequence-local mixers are tiled assuming one sequence"
    c_col = c.reshape(d_model, 1)

    for l in range(depth):
        b_ada_row = b_ada[l].reshape(1, -1)
        mod_a, cact_b = _ada(c_col, w_ada[l], b_ada_row, 2 * d_model)
        shift1, scale1 = mod_a[:, :d_model], mod_a[:, d_model:]

        h1 = _norm_mod(xs, norm1_g[l].reshape(1, -1), scale1, shift1)
        proj, w_ff1_bf, w_ff2_bf = _in_proj(h1, w_in[l], w_ff1[l], w_ff2[l])

        bs_full = jnp.repeat(jnp.transpose(b_spatial[l]), HEAD_DIM, axis=1)
        ya, yb, w_out_bf, w_up_a_bf, w_up_b_bf = _mixer(
            proj, ln_v_g[l].reshape(1, -1), ln_v_b[l].reshape(1, -1), w_spatial[l], bs_full,
            w_pool[l].astype(BF16), b_pool[l].reshape(1, -1), pool_scale[l].reshape(1, -1),
            w_out[l], w_up_a[l], w_up_b[l], d_a, d_b)

        merged, mod_b = _merge(ya, yb, w_up_a_bf, w_up_b_bf, proj,
                               b_gate[l, 0].reshape(1, -1), b_gate[l, 1].reshape(1, -1),
                               w_ada[l], cact_b, b_ada_row, 2 * d_model, d_model)
        gate1, shift2, scale2, gate2 = [
            mod_b[:, k * d_model:(k + 1) * d_model] for k in range(4)]
        x1 = _out_proj(merged, w_out_bf, xs, gate1)

        assert depth == 1, "the final norm is fused into the last layer's channel mixer"
        xs = _ffn(x1, norm2_g[l].reshape(1, -1), scale2, shift2, gate2,
                  norm_f_g.reshape(1, -1), w_ff1_bf, w_ff2_bf)

    return xs.reshape(batch, seq, d_model)
```

```python
import functools

import jax
import jax.numpy as jnp
from jax import lax
from jax.experimental import pallas as pl
from jax.experimental.pallas import tpu as pltpu

F32 = jnp.float32
BF16 = jnp.bfloat16

EPS = 1e-6
CHUNK = 128
HEAD_DIM = 128
POOL_WINDOWS = (2, 4, 8, 16)
HALO = 16
LANES = 128
MIB = 1024 * 1024


def _params(sem, vmem_mib):
    return pltpu.CompilerParams(dimension_semantics=sem, vmem_limit_bytes=vmem_mib * MIB)


def _gelu_tanh(x):
    c = 0.7978845608028654
    half = 0.5 * x
    return half + half * jnp.tanh(x * (c + (c * 0.044715) * (x * x)))


def _matvec_cols(w_ref, cb_ref, b_ref, o_ref):
    cb = cb_ref[...]
    for n0 in range(0, o_ref.shape[1], LANES):
        cols = slice(n0, n0 + LANES)
        o_ref[:, cols] = jnp.sum(w_ref[:, cols] * cb, axis=0, keepdims=True) + b_ref[:, cols]


def _ada_kernel(c_ref, w_ref, b_ref, o_ref, cb_ref):
    @pl.when(pl.program_id(0) == 0)
    def _():
        c = c_ref[...]
        cb_ref[...] = jnp.broadcast_to(c * jax.nn.sigmoid(c), cb_ref.shape)
    _matvec_cols(w_ref, cb_ref, b_ref, o_ref)


def _ada(c_col, w_ada, b_ada, n_cols, tn=512):
    d, _ = w_ada.shape
    return pl.pallas_call(
        _ada_kernel,
        grid=(n_cols // tn,),
        in_specs=[pl.BlockSpec((d, 1), lambda j: (0, 0)),
                  pl.BlockSpec((d, tn), lambda j: (0, j)),
                  pl.BlockSpec((1, tn), lambda j: (0, j))],
        out_specs=[pl.BlockSpec((1, tn), lambda j: (0, j)),
                   pl.BlockSpec((d, LANES), lambda j: (0, 0))],
        out_shape=[jax.ShapeDtypeStruct((1, n_cols), F32),
                   jax.ShapeDtypeStruct((d, LANES), F32)],
        compiler_params=_params(("arbitrary",), 40),
        name="ada",
    )(c_col, w_ada, b_ada)


def _norm_mod_kernel(x_ref, g_ref, sc_ref, sh_ref, o_ref):
    x = x_ref[...]
    ms = jnp.mean(x * x, axis=-1, keepdims=True)
    w = g_ref[...] * (1.0 + sc_ref[...])
    o_ref[...] = (x * lax.rsqrt(ms + EPS) * w + sh_ref[...]).astype(o_ref.dtype)


def _norm_mod(x, g, scale, shift, tr=256):
    s, d = x.shape
    vec = pl.BlockSpec((1, d), lambda i: (0, 0))
    return pl.pallas_call(
        _norm_mod_kernel,
        grid=(s // tr,),
        in_specs=[pl.BlockSpec((tr, d), lambda i: (i, 0)), vec, vec, vec],
        out_specs=pl.BlockSpec((tr, d), lambda i: (i, 0)),
        out_shape=jax.ShapeDtypeStruct((s, d), BF16),
        compiler_params=_params(("parallel",), 40),
        name="norm_mod",
    )(x, g, scale, shift)


CAST_BLK = 128


def _cast_ff_block(f1_ref, f2_ref, f1o_ref, f2o_ref):
    f1o_ref[...] = f1_ref[...].astype(BF16)
    f2o_ref[...] = f2_ref[...].astype(BF16)


def _cast_ff_specs(d, first, n_here, step_of):
    def idx(*g):
        return first + jnp.minimum(step_of(*g), n_here - 1)
    return [pl.BlockSpec((d, CAST_BLK), lambda *g: (0, idx(*g))),
            pl.BlockSpec((CAST_BLK, d), lambda *g: (idx(*g), 0))]


def _in_proj_kernel(h_ref, w_ref, f1_ref, f2_ref, o_ref, f1o_ref, f2o_ref):
    _cast_ff_block(f1_ref, f2_ref, f1o_ref, f2o_ref)
    o_ref[...] = jnp.dot(h_ref[...], w_ref[...].astype(BF16),
                         preferred_element_type=F32).astype(o_ref.dtype)


def _in_proj(h, w, w_ff1, w_ff2, tm=1024, tn=512):
    s, k = h.shape
    _, n = w.shape
    d, f = w_ff1.shape
    n_j = n // tn
    n_steps = (s // tm) * n_j
    n_cast = f // CAST_BLK
    assert n_cast <= n_steps <= 2 * n_cast, "every block of w_ff1 / w_ff2 must get a grid step"
    n_slow = n_steps - n_cast

    def cast_pos(i, j):
        step = i * n_j + j
        return jnp.where(step < 2 * n_slow, step // 2, step - n_slow)

    cast_specs = _cast_ff_specs(d, 0, n_cast, cast_pos)
    outs = pl.pallas_call(
        _in_proj_kernel,
        grid=(s // tm, n_j),
        in_specs=[pl.BlockSpec((tm, k), lambda i, j: (i, 0)),
                  pl.BlockSpec((k, tn), lambda i, j: (0, j))] + cast_specs,
        out_specs=[pl.BlockSpec((tm, tn), lambda i, j: (i, j))] + cast_specs,
        out_shape=[jax.ShapeDtypeStruct((s, n), BF16),
                   jax.ShapeDtypeStruct((d, f), BF16),
                   jax.ShapeDtypeStruct((f, d), BF16)],
        compiler_params=_params(("arbitrary", "arbitrary"), 56),
        name="in_proj",
    )(h, w, w_ff1, w_ff2)
    return outs


def _mixer_kernel(u_ref, v_ref, p_ref, halo_ref, lng_ref, lnb_ref, ws_ref, bs_ref,
                  wp_ref, bp_ref, ps_ref, wo_ref, ua_ref, ub_ref,
                  ya_ref, yb_ref, woo_ref, uao_ref, ubo_ref,
                  wsm_ref, band_ref, vn_ref, e_ref, pooled_ref, *, tm, d_a, pool_gd):
    i = pl.program_id(0)
    n_heads = d_a // HEAD_DIM
    rc = 32

    woo_ref[...] = wo_ref[...].astype(BF16)
    uao_ref[...] = ua_ref[...].astype(BF16)
    ubo_ref[...] = ub_ref[...].astype(BF16)

    @pl.when(i == 0)
    def _():
        row = lax.broadcasted_iota(jnp.int32, (CHUNK, CHUNK), 0)
        col = lax.broadcasted_iota(jnp.int32, (CHUNK, CHUNK), 1)
        for h in range(n_heads):
            wsm_ref[h] = jnp.where(row >= col, ws_ref[h], 0.0).astype(BF16)
        t = lax.broadcasted_iota(jnp.int32, (CHUNK, 2 * CHUNK), 0) + CHUNK
        s = lax.broadcasted_iota(jnp.int32, (CHUNK, 2 * CHUNK), 1)
        for g, win in enumerate(POOL_WINDOWS):
            band_ref[g] = jnp.where((s <= t) & (s > t - win), 1.0, 0.0).astype(BF16)
        e_ref[0:CHUNK - HALO, :] = jnp.zeros((CHUNK - HALO, e_ref.shape[1]), BF16)

    def ln_body(r, carry):
        rows = pl.ds(pl.multiple_of(r * rc, rc), rc)
        v = _gelu_tanh(v_ref[rows, :].astype(F32))
        mu = jnp.mean(v, axis=-1, keepdims=True)
        vc = v - mu
        var = jnp.mean(vc * vc, axis=-1, keepdims=True)
        vn = vc * lax.rsqrt(var + EPS) * lng_ref[...] + lnb_ref[...]
        vn_ref[rows, :] = vn.astype(BF16)
        return carry
    lax.fori_loop(0, tm // rc, ln_body, 0, unroll=2)

    def sgu_body(c, carry):
        rows = pl.ds(pl.multiple_of(c * CHUNK, CHUNK), CHUNK)
        for h in range(n_heads):
            cols = slice(h * HEAD_DIM, (h + 1) * HEAD_DIM)
            mixed = jnp.dot(wsm_ref[h], vn_ref[rows, cols], preferred_element_type=F32)
            ug = _gelu_tanh(u_ref[rows, cols].astype(F32))
            ya_ref[rows, cols] = (ug * (mixed + bs_ref[:, cols])).astype(ya_ref.dtype)
        return carry
    lax.fori_loop(0, tm // CHUNK, sgu_body, 0)

    e_ref[CHUNK - HALO:CHUNK, :] = jnp.where(i > 0, halo_ref[...], jnp.zeros_like(halo_ref))
    e_ref[CHUNK:, :] = p_ref[...]

    for g, win in enumerate(POOL_WINDOWS):
        cols = slice(g * pool_gd, (g + 1) * pool_gd)
        for r0 in range(0, tm, CHUNK):
            wsum = jnp.dot(band_ref[g], e_ref[r0:r0 + 2 * CHUNK, cols], preferred_element_type=F32)
            center = p_ref[r0:r0 + CHUNK, cols].astype(F32)
            t = i * tm + r0 + lax.broadcasted_iota(jnp.int32, (CHUNK, 1), 0)
            inv_cnt = 1.0 / jnp.minimum(t + 1, win).astype(F32)
            pooled_ref[r0:r0 + CHUNK, cols] = (wsum * inv_cnt - center).astype(BF16)
        y = jnp.dot(pooled_ref[:, cols], wp_ref[g], preferred_element_type=F32)
        yb_ref[:, cols] = ((y + bp_ref[:, cols]) * ps_ref[:, cols]).astype(yb_ref.dtype)


def _mixer(proj, ln_g, ln_b, w_spatial, bs_full, w_pool, b_pool, pool_scale,
           w_out, w_up_a, w_up_b, d_a, d_b, tm=256, cast_rows=128):
    s, _ = proj.shape
    d_model = w_out.shape[1]
    n_steps = s // tm
    n_heads = d_a // HEAD_DIM
    n_groups = len(POOL_WINDOWS)
    pool_gd = d_b // n_groups
    assert d_a == d_b and tm % CHUNK == 0 and tm % HALO == 0
    assert max(w_out.shape[0], w_up_a.shape[0], w_up_b.shape[0]) <= cast_rows * n_steps
    halo_blocks = tm // HALO
    vec_a = pl.BlockSpec((1, d_a), lambda i: (0, 0))
    vec_b = pl.BlockSpec((1, d_b), lambda i: (0, 0))

    def cast_spec(w):
        last = w.shape[0] // cast_rows - 1
        return pl.BlockSpec((cast_rows, w.shape[1]), lambda i: (jnp.minimum(i, last), 0))

    kern = functools.partial(_mixer_kernel, tm=tm, d_a=d_a, pool_gd=pool_gd)
    return pl.pallas_call(
        kern,
        grid=(n_steps,),
        in_specs=[
            pl.BlockSpec((tm, d_a), lambda i: (i, 0)),
            pl.BlockSpec((tm, d_a), lambda i: (i, 1)),
            pl.BlockSpec((tm, d_b), lambda i: (i, 2)),
            pl.BlockSpec((HALO, d_b), lambda i: (jnp.maximum(i * halo_blocks - 1, 0), 2)),
            vec_a, vec_a,
            pl.BlockSpec((n_heads, CHUNK, CHUNK), lambda i: (0, 0, 0)),
            pl.BlockSpec((CHUNK, d_a), lambda i: (0, 0)),
            pl.BlockSpec((n_groups, pool_gd, pool_gd), lambda i: (0, 0, 0)),
            vec_b, vec_b,
            cast_spec(w_out), cast_spec(w_up_a), cast_spec(w_up_b),
        ],
        out_specs=[pl.BlockSpec((tm, d_a), lambda i: (i, 0)),
                   pl.BlockSpec((tm, d_b), lambda i: (i, 0)),
                   cast_spec(w_out), cast_spec(w_up_a), cast_spec(w_up_b)],
        out_shape=[jax.ShapeDtypeStruct((s, d_a), BF16),
                   jax.ShapeDtypeStruct((s, d_b), BF16),
                   jax.ShapeDtypeStruct(w_out.shape, BF16),
                   jax.ShapeDtypeStruct(w_up_a.shape, BF16),
                   jax.ShapeDtypeStruct(w_up_b.shape, BF16)],
        scratch_shapes=[pltpu.VMEM((n_heads, CHUNK, CHUNK), BF16),
                        pltpu.VMEM((n_groups, CHUNK, 2 * CHUNK), BF16),
                        pltpu.VMEM((tm, d_a), BF16),
                        pltpu.VMEM((tm + CHUNK, d_b), BF16),
                        pltpu.VMEM((tm, d_b), BF16)],
        compiler_params=_params(("arbitrary",), 48),
        name="mixer",
    )(proj, proj, proj, proj, ln_g, ln_b, w_spatial, bs_full, w_pool, b_pool, pool_scale,
      w_out, w_up_a, w_up_b)


def _merge_kernel(ya_ref, yb_ref, wa_ref, wb_ref, ga_ref, gb_ref, bga_ref, bgb_ref,
                  wada_ref, cb_ref, bada_ref, o_ref, mod_ref):
    _matvec_cols(wada_ref, cb_ref, bada_ref, mod_ref)
    a = jnp.dot(ya_ref[...], wa_ref[...], preferred_element_type=F32)
    g_a = jax.nn.sigmoid(ga_ref[...].astype(F32) + bga_ref[...])
    m = g_a * a
    b = jnp.dot(yb_ref[...], wb_ref[...], preferred_element_type=F32)
    g_b = jax.nn.sigmoid(gb_ref[...].astype(F32) + bgb_ref[...])
    o_ref[...] = (m + g_b * b).astype(o_ref.dtype)


def _merge(ya, yb, wa, wb, proj, bga, bgb, w_ada, cact_b, b_ada, ada_done, d_model,
           tm=1024, tn=512):
    s, ka = ya.shape
    _, kb = yb.shape
    d, e = w_ada.shape
    ga_off = (proj.shape[1] - 2 * d_model) // tn
    gb_off = (proj.shape[1] - d_model) // tn
    n_j = d_model // tn
    n_steps = (s // tm) * n_j
    ada_blk = (e - ada_done) // n_steps
    assert ada_blk * n_steps == e - ada_done and ada_blk % LANES == 0 and ada_done % ada_blk == 0
    ada_off = ada_done // ada_blk
    return pl.pallas_call(
        _merge_kernel,
        grid=(s // tm, n_j),
        in_specs=[pl.BlockSpec((tm, ka), lambda i, j: (i, 0)),
                  pl.BlockSpec((tm, kb), lambda i, j: (i, 0)),
                  pl.BlockSpec((ka, tn), lambda i, j: (0, j)),
                  pl.BlockSpec((kb, tn), lambda i, j: (0, j)),
                  pl.BlockSpec((tm, tn), lambda i, j: (i, ga_off + j)),
                  pl.BlockSpec((tm, tn), lambda i, j: (i, gb_off + j)),
                  pl.BlockSpec((1, tn), lambda i, j: (0, j)),
                  pl.BlockSpec((1, tn), lambda i, j: (0, j)),
                  pl.BlockSpec((d, ada_blk), lambda i, j: (0, ada_off + i * n_j + j)),
                  pl.BlockSpec((d, LANES), lambda i, j: (0, 0)),
                  pl.BlockSpec((1, ada_blk), lambda i, j: (0, ada_off + i * n_j + j))],
        out_specs=[pl.BlockSpec((tm, tn), lambda i, j: (i, j)),
                   pl.BlockSpec((1, ada_blk), lambda i, j: (0, i * n_j + j))],
        out_shape=[jax.ShapeDtypeStruct((s, d_model), BF16),
                   jax.ShapeDtypeStruct((1, e - ada_done), F32)],
        compiler_params=_params(("parallel", "parallel"), 56),
        name="merge",
    )(ya, yb, wa, wb, proj, proj, bga, bgb, w_ada, cact_b, b_ada)


def _out_proj_kernel(m_ref, w_ref, x_ref, gate_ref, o_ref):
    y = jnp.dot(m_ref[...], w_ref[...], preferred_element_type=F32)
    o_ref[...] = x_ref[...] + gate_ref[...] * y


def _out_proj(m, w, x, gate, tm=1024, tn=512):
    s, k = m.shape
    _, n = w.shape
    return pl.pallas_call(
        _out_proj_kernel,
        grid=(s // tm, n // tn),
        in_specs=[pl.BlockSpec((tm, k), lambda i, j: (i, 0)),
                  pl.BlockSpec((k, tn), lambda i, j: (0, j)),
                  pl.BlockSpec((tm, tn), lambda i, j: (i, j)),
                  pl.BlockSpec((1, tn), lambda i, j: (0, j))],
        out_specs=pl.BlockSpec((tm, tn), lambda i, j: (i, j)),
        out_shape=jax.ShapeDtypeStruct((s, n), F32),
        compiler_params=_params(("parallel", "parallel"), 52),
        name="out_proj",
    )(m, w, x, gate)


def _ffn_kernel(x_hbm, g2_ref, sc_ref, sh_ref, gate_ref, gf_ref, w1_hbm, w2_hbm, o_hbm,
                h_ref, acc_ref, r_ref, w1buf, w2buf, xbuf, obuf, sem1, sem2, semx, semo,
                *, tm, tf, n_blk, n_slots, n_xs, rc, nc, oc):
    i = pl.program_id(0)
    base = i * n_blk

    def w1_copy(g):
        blk = g % n_blk
        slot = g % n_slots
        return pltpu.make_async_copy(w1_hbm.at[:, pl.ds(blk * tf, tf)], w1buf.at[slot], sem1.at[slot])

    def w2_copy(g):
        blk = g % n_blk
        slot = g % n_slots
        return pltpu.make_async_copy(w2_hbm.at[pl.ds(blk * tf, tf), :], w2buf.at[slot], sem2.at[slot])

    def up(g, par):
        a = jnp.dot(h_ref[...], w1buf[g % n_slots], preferred_element_type=F32)
        r = jnp.maximum(a, 0.0)
        r_ref[par] = (r * r).astype(BF16)

    def down(g, par):
        slot = g % n_slots
        for n0 in range(0, acc_ref.shape[1], nc):
            acc_ref[:, n0:n0 + nc] += jnp.dot(
                r_ref[par], w2buf[slot, :, n0:n0 + nc], preferred_element_type=F32)

    @pl.when(i == 0)
    def _():
        for g in range(n_slots):
            w1_copy(g).start()
            w2_copy(g).start()

    n_out = tm // oc

    def x_copy(c):
        slot = c % n_xs
        return pltpu.make_async_copy(
            x_hbm.at[pl.ds(i * tm + c * oc, oc), :], xbuf.at[slot], semx.at[slot])

    def x_stream_begin():
        for c in range(n_xs - 1):
            x_copy(c).start()

    def x_stream_next(c):
        x_copy(c).wait()

        @pl.when(c + n_xs - 1 < n_out)
        def _():
            x_copy(c + n_xs - 1).start()

    w = g2_ref[...] * (1.0 + sc_ref[...])
    sh = sh_ref[...]
    x_stream_begin()

    def pro_body(c, carry):
        x_stream_next(c)
        slot = c % n_xs
        for r0 in range(0, oc, rc):
            rows = pl.ds(pl.multiple_of(c * oc + r0, rc), rc)
            x = xbuf[slot, r0:r0 + rc, :]
            ms = jnp.mean(x * x, axis=-1, keepdims=True)
            h_ref[rows, :] = (x * lax.rsqrt(ms + EPS) * w + sh).astype(BF16)
            acc_ref[rows, :] = jnp.zeros((rc, acc_ref.shape[1]), F32)
        return carry
    lax.fori_loop(0, n_out, pro_body, 0)

    w1_copy(base).wait()
    up(base, 0)
    w1_copy(base + n_slots).start()

    def blk_body(k, carry):
        g = base + k
        par = k % 2
        w2_copy(g).wait()
        w1_copy(g + 1).wait()
        down(g, par)
        up(g + 1, 1 - par)
        w2_copy(g + n_slots).start()
        w1_copy(g + 1 + n_slots).start()
        return carry
    lax.fori_loop(0, n_blk - 1, blk_body, 0)

    g_last = base + n_blk - 1
    x_stream_begin()
    w2_copy(g_last).wait()
    down(g_last, (n_blk - 1) % 2)
    w2_copy(g_last + n_slots).start()

    gate = gate_ref[...]
    gf = gf_ref[...]

    def out_copy(c, slot):
        return pltpu.make_async_copy(
            obuf.at[slot], o_hbm.at[pl.ds(i * tm + c * oc, oc), :], semo.at[slot])

    def epi_body(c, carry):
        slot = c % 2
        xslot = c % n_xs
        x_stream_next(c)

        @pl.when(c >= 2)
        def _():
            out_copy(c - 2, slot).wait()

        for r0 in range(0, oc, rc):
            rows = pl.ds(pl.multiple_of(c * oc + r0, rc), rc)
            x2 = xbuf[xslot, r0:r0 + rc, :] + gate * acc_ref[rows, :]
            ms = jnp.mean(x2 * x2, axis=-1, keepdims=True)
            obuf[slot, r0:r0 + rc, :] = x2 * lax.rsqrt(ms + EPS) * gf
        out_copy(c, slot).start()
        return carry
    lax.fori_loop(0, n_out, epi_body, 0)
    out_copy(n_out - 2, n_out % 2).wait()
    out_copy(n_out - 1, (n_out - 1) % 2).wait()

    @pl.when(i == pl.num_programs(0) - 1)
    def _():
        end = base + n_blk
        for g in range(n_slots):
            w1_copy(end + g).wait()
            w2_copy(end + g).wait()


def _ffn(x1, g2, scale, shift, gate, gf, w1, w2, tm=1024, tf=512, n_slots=2, n_xs=3, oc=128):
    s, d = x1.shape
    _, f = w1.shape
    n_blk = f // tf
    assert n_blk % 2 == 0 and n_blk > n_slots and (tm // oc) >= n_xs
    vec = pl.BlockSpec((1, d), lambda i: (0, 0))
    hbm = pl.BlockSpec(memory_space=pl.ANY)
    kern = functools.partial(_ffn_kernel, tm=tm, tf=tf, n_blk=n_blk, n_slots=n_slots,
                             n_xs=n_xs, rc=16, nc=1024, oc=oc)
    return pl.pallas_call(
        kern,
        grid=(s // tm,),
        in_specs=[hbm, vec, vec, vec, vec, vec, hbm, hbm],
        out_specs=hbm,
        out_shape=jax.ShapeDtypeStruct((s, d), F32),
        scratch_shapes=[pltpu.VMEM((tm, d), BF16),
                        pltpu.VMEM((tm, d), F32),
                        pltpu.VMEM((2, tm, tf), BF16),
                        pltpu.VMEM((n_slots, d, tf), BF16),
                        pltpu.VMEM((n_slots, tf, d), BF16),
                        pltpu.VMEM((n_xs, oc, d), F32),
                        pltpu.VMEM((2, oc, d), F32),
                        pltpu.SemaphoreType.DMA((n_slots,)),
                        pltpu.SemaphoreType.DMA((n_slots,)),
                        pltpu.SemaphoreType.DMA((n_xs,)),
                        pltpu.SemaphoreType.DMA((2,))],
        compiler_params=_params(("arbitrary",), 58),
        name="ffn",
    )(x1, g2, scale, shift, gate, gf, w1, w2)


def kernel(x, c, w_ada, b_ada, norm1_g, w_in, ln_v_g, ln_v_b, w_spatial, b_spatial, w_pool,
           b_pool, pool_scale, b_gate, w_up_a, w_up_b, w_out, norm2_g, w_ff1, w_ff2, norm_f_g):
    batch, seq, d_model = x.shape
    depth = w_ada.shape[0]
    d_a = ln_v_g.shape[-1]
    d_b = pool_scale.shape[-1]
    xs = x.reshape(batch * seq, d_model)
    assert batch == 1, "sequence-local mixers are tiled assuming one sequence"
    c_col = c.reshape(d_model, 1)

    for l in range(depth):
        b_ada_row = b_ada[l].reshape(1, -1)
        mod_a, cact_b = _ada(c_col, w_ada[l], b_ada_row, 2 * d_model)
        shift1, scale1 = mod_a[:, :d_model], mod_a[:, d_model:]

        h1 = _norm_mod(xs, norm1_g[l].reshape(1, -1), scale1, shift1)
        proj, w_ff1_bf, w_ff2_bf = _in_proj(h1, w_in[l], w_ff1[l], w_ff2[l])

        bs_full = jnp.repeat(jnp.transpose(b_spatial[l]), HEAD_DIM, axis=1)
        ya, yb, w_out_bf, w_up_a_bf, w_up_b_bf = _mixer(
            proj, ln_v_g[l].reshape(1, -1), ln_v_b[l].reshape(1, -1), w_spatial[l], bs_full,
            w_pool[l].astype(BF16), b_pool[l].reshape(1, -1), pool_scale[l].reshape(1, -1),
            w_out[l], w_up_a[l], w_up_b[l], d_a, d_b)

        merged, mod_b = _merge(ya, yb, w_up_a_bf, w_up_b_bf, proj,
                               b_gate[l, 0].reshape(1, -1), b_gate[l, 1].reshape(1, -1),
                               w_ada[l], cact_b, b_ada_row, 2 * d_model, d_model)
        gate1, shift2, scale2, gate2 = [
            mod_b[:, k * d_model:(k + 1) * d_model] for k in range(4)]
        x1 = _out_proj(merged, w_out_bf, xs, gate1)

        assert depth == 1, "the final norm is fused into the last layer's channel mixer"
        xs = _ffn(x1, norm2_g[l].reshape(1, -1), scale2, shift2, gate2,
                  norm_f_g.reshape(1, -1), w_ff1_bf, w_ff2_bf)

    return xs.reshape(batch, seq, d_model)
```

```python
import functools

import jax
import jax.numpy as jnp
from jax import lax
from jax.experimental import pallas as pl
from jax.experimental.pallas import tpu as pltpu

F32 = jnp.float32
BF16 = jnp.bfloat16

EPS = 1e-6
CHUNK = 128
HEAD_DIM = 128
POOL_WINDOWS = (2, 4, 8, 16)
HALO = 16
LANES = 128
MIB = 1024 * 1024


def _params(sem, vmem_mib):
    return pltpu.CompilerParams(dimension_semantics=sem, vmem_limit_bytes=vmem_mib * MIB)


def _gelu_tanh(x):
    c = 0.7978845608028654
    half = 0.5 * x
    return half + half * jnp.tanh(x * (c + (c * 0.044715) * (x * x)))


def _matvec_cols(w_ref, cb_ref, b_ref, o_ref):
    cb = cb_ref[...]
    for n0 in range(0, o_ref.shape[1], LANES):
        cols = slice(n0, n0 + LANES)
        o_ref[:, cols] = jnp.sum(w_ref[:, cols] * cb, axis=0, keepdims=True) + b_ref[:, cols]


def _ada_kernel(c_ref, w_ref, b_ref, o_ref, cb_ref):
    @pl.when(pl.program_id(0) == 0)
    def _():
        c = c_ref[...]
        cb_ref[...] = jnp.broadcast_to(c * jax.nn.sigmoid(c), cb_ref.shape)
    _matvec_cols(w_ref, cb_ref, b_ref, o_ref)


def _ada(c_col, w_ada, b_ada, n_cols, tn=512):
    d, _ = w_ada.shape
    return pl.pallas_call(
        _ada_kernel,
        grid=(n_cols // tn,),
        in_specs=[pl.BlockSpec((d, 1), lambda j: (0, 0)),
                  pl.BlockSpec((d, tn), lambda j: (0, j)),
                  pl.BlockSpec((1, tn), lambda j: (0, j))],
        out_specs=[pl.BlockSpec((1, tn), lambda j: (0, j)),
                   pl.BlockSpec((d, LANES), lambda j: (0, 0))],
        out_shape=[jax.ShapeDtypeStruct((1, n_cols), F32),
                   jax.ShapeDtypeStruct((d, LANES), F32)],
        compiler_params=_params(("arbitrary",), 40),
        name="ada",
    )(c_col, w_ada, b_ada)


ROW_CHUNK = 16


class _RowStream:
    def __init__(self, x_hbm, row0, buf, sem, n_chunks):
        self.x_hbm, self.row0, self.buf, self.sem, self.n_chunks = x_hbm, row0, buf, sem, n_chunks
        self.slots, self.rows = buf.shape[0], buf.shape[1]

    def _copy(self, c):
        slot = c % self.slots
        src = self.x_hbm.at[pl.ds(self.row0 + c * self.rows, self.rows), :]
        return pltpu.make_async_copy(src, self.buf.at[slot], self.sem.at[slot])

    def begin(self):
        for c in range(self.slots - 1):
            self._copy(c).start()

    def next(self, c):
        self._copy(c).wait()

        @pl.when(c + self.slots - 1 < self.n_chunks)
        def _():
            self._copy(c + self.slots - 1).start()
        return c % self.slots


def _norm_mod_rows(x, w, shift):
    ms = jnp.mean(x * x, axis=-1, keepdims=True)
    return (x * lax.rsqrt(ms + EPS) * w + shift).astype(BF16)


CAST_BLK = 128


def _cast_ff_block(f1_ref, f2_ref, f1o_ref, f2o_ref):
    f1o_ref[...] = f1_ref[...].astype(BF16)
    f2o_ref[...] = f2_ref[...].astype(BF16)


def _cast_ff_specs(d, first, n_here, step_of):
    def idx(*g):
        return first + jnp.minimum(step_of(*g), n_here - 1)
    return [pl.BlockSpec((d, CAST_BLK), lambda *g: (0, idx(*g))),
            pl.BlockSpec((CAST_BLK, d), lambda *g: (idx(*g), 0))]


def _in_proj_kernel(x_hbm, g_ref, sc_ref, sh_ref, w_ref, f1_ref, f2_ref,
                    o_ref, f1o_ref, f2o_ref, h_ref, xbuf, semx, *, tm):
    i = pl.program_id(0)

    @pl.when(pl.program_id(1) == 0)
    def _():
        w = g_ref[...] * (1.0 + sc_ref[...])
        sh = sh_ref[...]
        xc = xbuf.shape[1]
        stream = _RowStream(x_hbm, i * tm, xbuf, semx, tm // xc)
        stream.begin()

        def body(c, carry):
            slot = stream.next(c)
            for r0 in range(0, xc, ROW_CHUNK):
                rows = pl.ds(pl.multiple_of(c * xc + r0, ROW_CHUNK), ROW_CHUNK)
                h_ref[rows, :] = _norm_mod_rows(xbuf[slot, r0:r0 + ROW_CHUNK, :], w, sh)
            return carry
        lax.fori_loop(0, tm // xc, body, 0)

    _cast_ff_block(f1_ref, f2_ref, f1o_ref, f2o_ref)
    o_ref[...] = jnp.dot(h_ref[...], w_ref[...].astype(BF16),
                         preferred_element_type=F32).astype(o_ref.dtype)


def _in_proj(x, g, scale, shift, w, w_ff1, w_ff2, tm=1024, tn=512, xc=128, x_slots=3):
    s, k = x.shape
    _, n = w.shape
    d, f = w_ff1.shape
    n_j = n // tn
    n_steps = (s // tm) * n_j
    n_cast = f // CAST_BLK
    assert n_cast <= n_steps <= 2 * n_cast, "every block of w_ff1 / w_ff2 must get a grid step"
    n_slow = n_steps - n_cast

    def cast_pos(i, j):
        step = i * n_j + j
        return jnp.where(step < 2 * n_slow, step // 2, step - n_slow)

    cast_specs = _cast_ff_specs(d, 0, n_cast, cast_pos)
    vec = pl.BlockSpec((1, k), lambda i, j: (0, 0))
    return pl.pallas_call(
        functools.partial(_in_proj_kernel, tm=tm),
        grid=(s // tm, n_j),
        in_specs=[pl.BlockSpec(memory_space=pl.ANY), vec, vec, vec,
                  pl.BlockSpec((k, tn), lambda i, j: (0, j))] + cast_specs,
        out_specs=[pl.BlockSpec((tm, tn), lambda i, j: (i, j))] + cast_specs,
        out_shape=[jax.ShapeDtypeStruct((s, n), BF16),
                   jax.ShapeDtypeStruct((d, f), BF16),
                   jax.ShapeDtypeStruct((f, d), BF16)],
        scratch_shapes=[pltpu.VMEM((tm, k), BF16),
                        pltpu.VMEM((x_slots, xc, k), F32),
                        pltpu.SemaphoreType.DMA((x_slots,))],
        compiler_params=_params(("arbitrary", "arbitrary"), 56),
        name="in_proj",
    )(x, g, scale, shift, w, w_ff1, w_ff2)


def _mixer_kernel(u_ref, v_ref, p_ref, halo_ref, lng_ref, lnb_ref, ws_ref, bs_ref,
                  wp_ref, bp_ref, ps_ref, wo_ref, ua_ref, ub_ref,
                  ya_ref, yb_ref, woo_ref, uao_ref, ubo_ref,
                  wsm_ref, band_ref, vn_ref, e_ref, pooled_ref, *, tm, d_a, pool_gd):
    i = pl.program_id(0)
    n_heads = d_a // HEAD_DIM
    rc = 32

    woo_ref[...] = wo_ref[...].astype(BF16)
    uao_ref[...] = ua_ref[...].astype(BF16)
    ubo_ref[...] = ub_ref[...].astype(BF16)

    @pl.when(i == 0)
    def _():
        row = lax.broadcasted_iota(jnp.int32, (CHUNK, CHUNK), 0)
        col = lax.broadcasted_iota(jnp.int32, (CHUNK, CHUNK), 1)
        for h in range(n_heads):
            wsm_ref[h] = jnp.where(row >= col, ws_ref[h], 0.0).astype(BF16)
        t = lax.broadcasted_iota(jnp.int32, (CHUNK, 2 * CHUNK), 0) + CHUNK
        s = lax.broadcasted_iota(jnp.int32, (CHUNK, 2 * CHUNK), 1)
        for g, win in enumerate(POOL_WINDOWS):
            band_ref[g] = jnp.where((s <= t) & (s > t - win), 1.0, 0.0).astype(BF16)
        e_ref[0:CHUNK - HALO, :] = jnp.zeros((CHUNK - HALO, e_ref.shape[1]), BF16)

    def ln_body(r, carry):
        rows = pl.ds(pl.multiple_of(r * rc, rc), rc)
        v = _gelu_tanh(v_ref[rows, :].astype(F32))
        mu = jnp.mean(v, axis=-1, keepdims=True)
        vc = v - mu
        var = jnp.mean(vc * vc, axis=-1, keepdims=True)
        vn = vc * lax.rsqrt(var + EPS) * lng_ref[...] + lnb_ref[...]
        vn_ref[rows, :] = vn.astype(BF16)
        return carry
    lax.fori_loop(0, tm // rc, ln_body, 0, unroll=2)

    def sgu_body(c, carry):
        rows = pl.ds(pl.multiple_of(c * CHUNK, CHUNK), CHUNK)
        for h in range(n_heads):
            cols = slice(h * HEAD_DIM, (h + 1) * HEAD_DIM)
            mixed = jnp.dot(wsm_ref[h], vn_ref[rows, cols], preferred_element_type=F32)
            ug = _gelu_tanh(u_ref[rows, cols].astype(F32))
            ya_ref[rows, cols] = (ug * (mixed + bs_ref[:, cols])).astype(ya_ref.dtype)
        return carry
    lax.fori_loop(0, tm // CHUNK, sgu_body, 0)

    e_ref[CHUNK - HALO:CHUNK, :] = jnp.where(i > 0, halo_ref[...], jnp.zeros_like(halo_ref))
    e_ref[CHUNK:, :] = p_ref[...]

    for g, win in enumerate(POOL_WINDOWS):
        cols = slice(g * pool_gd, (g + 1) * pool_gd)
        for r0 in range(0, tm, CHUNK):
            wsum = jnp.dot(band_ref[g], e_ref[r0:r0 + 2 * CHUNK, cols], preferred_element_type=F32)
            center = p_ref[r0:r0 + CHUNK, cols].astype(F32)
            t = i * tm + r0 + lax.broadcasted_iota(jnp.int32, (CHUNK, 1), 0)
            inv_cnt = 1.0 / jnp.minimum(t + 1, win).astype(F32)
            pooled_ref[r0:r0 + CHUNK, cols] = (wsum * inv_cnt - center).astype(BF16)
        y = jnp.dot(pooled_ref[:, cols], wp_ref[g], preferred_element_type=F32)
        yb_ref[:, cols] = ((y + bp_ref[:, cols]) * ps_ref[:, cols]).astype(yb_ref.dtype)


def _mixer(proj, ln_g, ln_b, w_spatial, bs_full, w_pool, b_pool, pool_scale,
           w_out, w_up_a, w_up_b, d_a, d_b, tm=256, cast_rows=128):
    s, _ = proj.shape
    d_model = w_out.shape[1]
    n_steps = s // tm
    n_heads = d_a // HEAD_DIM
    n_groups = len(POOL_WINDOWS)
    pool_gd = d_b // n_groups
    assert d_a == d_b and tm % CHUNK == 0 and tm % HALO == 0
    assert max(w_out.shape[0], w_up_a.shape[0], w_up_b.shape[0]) <= cast_rows * n_steps
    halo_blocks = tm // HALO
    vec_a = pl.BlockSpec((1, d_a), lambda i: (0, 0))
    vec_b = pl.BlockSpec((1, d_b), lambda i: (0, 0))

    def cast_spec(w):
        last = w.shape[0] // cast_rows - 1
        return pl.BlockSpec((cast_rows, w.shape[1]), lambda i: (jnp.minimum(i, last), 0))

    kern = functools.partial(_mixer_kernel, tm=tm, d_a=d_a, pool_gd=pool_gd)
    return pl.pallas_call(
        kern,
        grid=(n_steps,),
        in_specs=[
            pl.BlockSpec((tm, d_a), lambda i: (i, 0)),
            pl.BlockSpec((tm, d_a), lambda i: (i, 1)),
            pl.BlockSpec((tm, d_b), lambda i: (i, 2)),
            pl.BlockSpec((HALO, d_b), lambda i: (jnp.maximum(i * halo_blocks - 1, 0), 2)),
            vec_a, vec_a,
            pl.BlockSpec((n_heads, CHUNK, CHUNK), lambda i: (0, 0, 0)),
            pl.BlockSpec((CHUNK, d_a), lambda i: (0, 0)),
            pl.BlockSpec((n_groups, pool_gd, pool_gd), lambda i: (0, 0, 0)),
            vec_b, vec_b,
            cast_spec(w_out), cast_spec(w_up_a), cast_spec(w_up_b),
        ],
        out_specs=[pl.BlockSpec((tm, d_a), lambda i: (i, 0)),
                   pl.BlockSpec((tm, d_b), lambda i: (i, 0)),
                   cast_spec(w_out), cast_spec(w_up_a), cast_spec(w_up_b)],
        out_shape=[jax.ShapeDtypeStruct((s, d_a), BF16),
                   jax.ShapeDtypeStruct((s, d_b), BF16),
                   jax.ShapeDtypeStruct(w_out.shape, BF16),
                   jax.ShapeDtypeStruct(w_up_a.shape, BF16),
                   jax.ShapeDtypeStruct(w_up_b.shape, BF16)],
        scratch_shapes=[pltpu.VMEM((n_heads, CHUNK, CHUNK), BF16),
                        pltpu.VMEM((n_groups, CHUNK, 2 * CHUNK), BF16),
                        pltpu.VMEM((tm, d_a), BF16),
                        pltpu.VMEM((tm + CHUNK, d_b), BF16),
                        pltpu.VMEM((tm, d_b), BF16)],
        compiler_params=_params(("arbitrary",), 48),
        name="mixer",
    )(proj, proj, proj, proj, ln_g, ln_b, w_spatial, bs_full, w_pool, b_pool, pool_scale,
      w_out, w_up_a, w_up_b)


def _merge_kernel(ya_ref, yb_ref, wa_ref, wb_ref, ga_ref, gb_ref, bga_ref, bgb_ref,
                  wada_ref, cb_ref, bada_ref, o_ref, mod_ref):
    _matvec_cols(wada_ref, cb_ref, bada_ref, mod_ref)
    a = jnp.dot(ya_ref[...], wa_ref[...], preferred_element_type=F32)
    g_a = jax.nn.sigmoid(ga_ref[...].astype(F32) + bga_ref[...])
    m = g_a * a
    b = jnp.dot(yb_ref[...], wb_ref[...], preferred_element_type=F32)
    g_b = jax.nn.sigmoid(gb_ref[...].astype(F32) + bgb_ref[...])
    o_ref[...] = (m + g_b * b).astype(o_ref.dtype)


def _merge(ya, yb, wa, wb, proj, bga, bgb, w_ada, cact_b, b_ada, ada_done, d_model,
           tm=1024, tn=512):
    s, ka = ya.shape
    _, kb = yb.shape
    d, e = w_ada.shape
    ga_off = (proj.shape[1] - 2 * d_model) // tn
    gb_off = (proj.shape[1] - d_model) // tn
    n_j = d_model // tn
    n_steps = (s // tm) * n_j
    ada_blk = (e - ada_done) // n_steps
    assert ada_blk * n_steps == e - ada_done and ada_blk % LANES == 0 and ada_done % ada_blk == 0
    ada_off = ada_done // ada_blk
    return pl.pallas_call(
        _merge_kernel,
        grid=(s // tm, n_j),
        in_specs=[pl.BlockSpec((tm, ka), lambda i, j: (i, 0)),
                  pl.BlockSpec((tm, kb), lambda i, j: (i, 0)),
                  pl.BlockSpec((ka, tn), lambda i, j: (0, j)),
                  pl.BlockSpec((kb, tn), lambda i, j: (0, j)),
                  pl.BlockSpec((tm, tn), lambda i, j: (i, ga_off + j)),
                  pl.BlockSpec((tm, tn), lambda i, j: (i, gb_off + j)),
                  pl.BlockSpec((1, tn), lambda i, j: (0, j)),
                  pl.BlockSpec((1, tn), lambda i, j: (0, j)),
                  pl.BlockSpec((d, ada_blk), lambda i, j: (0, ada_off + i * n_j + j)),
                  pl.BlockSpec((d, LANES), lambda i, j: (0, 0)),
                  pl.BlockSpec((1, ada_blk), lambda i, j: (0, ada_off + i * n_j + j))],
        out_specs=[pl.BlockSpec((tm, tn), lambda i, j: (i, j)),
                   pl.BlockSpec((1, ada_blk), lambda i, j: (0, i * n_j + j))],
        out_shape=[jax.ShapeDtypeStruct((s, d_model), BF16),
                   jax.ShapeDtypeStruct((1, e - ada_done), F32)],
        compiler_params=_params(("parallel", "parallel"), 56),
        name="merge",
    )(ya, yb, wa, wb, proj, proj, bga, bgb, w_ada, cact_b, b_ada)


def _out_proj_kernel(m_ref, w_ref, x_ref, gate_ref, o_ref):
    y = jnp.dot(m_ref[...], w_ref[...], preferred_element_type=F32)
    o_ref[...] = x_ref[...] + gate_ref[...] * y


def _out_proj(m, w, x, gate, tm=1024, tn=512):
    s, k = m.shape
    _, n = w.shape
    return pl.pallas_call(
        _out_proj_kernel,
        grid=(s // tm, n // tn),
        in_specs=[pl.BlockSpec((tm, k), lambda i, j: (i, 0)),
                  pl.BlockSpec((k, tn), lambda i, j: (0, j)),
                  pl.BlockSpec((tm, tn), lambda i, j: (i, j)),
                  pl.BlockSpec((1, tn), lambda i, j: (0, j))],
        out_specs=pl.BlockSpec((tm, tn), lambda i, j: (i, j)),
        out_shape=jax.ShapeDtypeStruct((s, n), F32),
        compiler_params=_params(("parallel", "parallel"), 52),
        name="out_proj",
    )(m, w, x, gate)


def _ffn_kernel(x_hbm, g2_ref, sc_ref, sh_ref, gate_ref, gf_ref, w1_hbm, w2_hbm, o_hbm,
                h_ref, acc_ref, r_ref, w1buf, w2buf, xbuf, obuf, sem1, sem2, semx, semo,
                *, tm, tf, n_blk, n_slots, rc, nc, oc):
    i = pl.program_id(0)
    base = i * n_blk

    def w1_copy(g):
        blk = g % n_blk
        slot = g % n_slots
        return pltpu.make_async_copy(w1_hbm.at[:, pl.ds(blk * tf, tf)], w1buf.at[slot], sem1.at[slot])

    def w2_copy(g):
        blk = g % n_blk
        slot = g % n_slots
        return pltpu.make_async_copy(w2_hbm.at[pl.ds(blk * tf, tf), :], w2buf.at[slot], sem2.at[slot])

    def up(g, par):
        a = jnp.dot(h_ref[...], w1buf[g % n_slots], preferred_element_type=F32)
        r = jnp.maximum(a, 0.0)
        r_ref[par] = (r * r).astype(BF16)

    def down(g, par):
        slot = g % n_slots
        for n0 in range(0, acc_ref.shape[1], nc):
            acc_ref[:, n0:n0 + nc] += jnp.dot(
                r_ref[par], w2buf[slot, :, n0:n0 + nc], preferred_element_type=F32)

    @pl.when(i == 0)
    def _():
        for g in range(n_slots):
            w1_copy(g).start()
            w2_copy(g).start()

    n_out = tm // oc
    stream = _RowStream(x_hbm, i * tm, xbuf, semx, n_out)

    w = g2_ref[...] * (1.0 + sc_ref[...])
    sh = sh_ref[...]
    stream.begin()

    def pro_body(c, carry):
        slot = stream.next(c)
        for r0 in range(0, oc, rc):
            rows = pl.ds(pl.multiple_of(c * oc + r0, rc), rc)
            h_ref[rows, :] = _norm_mod_rows(xbuf[slot, r0:r0 + rc, :], w, sh)
            acc_ref[rows, :] = jnp.zeros((rc, acc_ref.shape[1]), F32)
        return carry
    lax.fori_loop(0, n_out, pro_body, 0)

    w1_copy(base).wait()
    up(base, 0)
    w1_copy(base + n_slots).start()

    def blk_body(k, carry):
        g = base + k
        par = k % 2
        w2_copy(g).wait()
        w1_copy(g + 1).wait()
        down(g, par)
        up(g + 1, 1 - par)
        w2_copy(g + n_slots).start()
        w1_copy(g + 1 + n_slots).start()
        return carry
    lax.fori_loop(0, n_blk - 1, blk_body, 0)

    g_last = base + n_blk - 1
    stream.begin()
    w2_copy(g_last).wait()
    down(g_last, (n_blk - 1) % 2)
    w2_copy(g_last + n_slots).start()

    gate = gate_ref[...]
    gf = gf_ref[...]

    def out_copy(c, slot):
        return pltpu.make_async_copy(
            obuf.at[slot], o_hbm.at[pl.ds(i * tm + c * oc, oc), :], semo.at[slot])

    def epi_body(c, carry):
        slot = c % 2
        xslot = stream.next(c)

        @pl.when(c >= 2)
        def _():
            out_copy(c - 2, slot).wait()

        for r0 in range(0, oc, rc):
            rows = pl.ds(pl.multiple_of(c * oc + r0, rc), rc)
            x2 = xbuf[xslot, r0:r0 + rc, :] + gate * acc_ref[rows, :]
            ms = jnp.mean(x2 * x2, axis=-1, keepdims=True)
            obuf[slot, r0:r0 + rc, :] = x2 * lax.rsqrt(ms + EPS) * gf
        out_copy(c, slot).start()
        return carry
    lax.fori_loop(0, n_out, epi_body, 0)
    out_copy(n_out - 2, n_out % 2).wait()
    out_copy(n_out - 1, (n_out - 1) % 2).wait()

    @pl.when(i == pl.num_programs(0) - 1)
    def _():
        end = base + n_blk
        for g in range(n_slots):
            w1_copy(end + g).wait()
            w2_copy(end + g).wait()


def _ffn(x1, g2, scale, shift, gate, gf, w1, w2, tm=1024, tf=512, n_slots=2, n_xs=3, oc=128):
    s, d = x1.shape
    _, f = w1.shape
    n_blk = f // tf
    assert n_blk % 2 == 0 and n_blk > n_slots and (tm // oc) >= n_xs
    vec = pl.BlockSpec((1, d), lambda i: (0, 0))
    hbm = pl.BlockSpec(memory_space=pl.ANY)
    kern = functools.partial(_ffn_kernel, tm=tm, tf=tf, n_blk=n_blk, n_slots=n_slots,
                             rc=ROW_CHUNK, nc=1024, oc=oc)
    return pl.pallas_call(
        kern,
        grid=(s // tm,),
        in_specs=[hbm, vec, vec, vec, vec, vec, hbm, hbm],
        out_specs=hbm,
        out_shape=jax.ShapeDtypeStruct((s, d), F32),
        scratch_shapes=[pltpu.VMEM((tm, d), BF16),
                        pltpu.VMEM((tm, d), F32),
                        pltpu.VMEM((2, tm, tf), BF16),
                        pltpu.VMEM((n_slots, d, tf), BF16),
                        pltpu.VMEM((n_slots, tf, d), BF16),
                        pltpu.VMEM((n_xs, oc, d), F32),
                        pltpu.VMEM((2, oc, d), F32),
                        pltpu.SemaphoreType.DMA((n_slots,)),
                        pltpu.SemaphoreType.DMA((n_slots,)),
                        pltpu.SemaphoreType.DMA((n_xs,)),
                        pltpu.SemaphoreType.DMA((2,))],
        compiler_params=_params(("arbitrary",), 58),
        name="ffn",
    )(x1, g2, scale, shift, gate, gf, w1, w2)


def kernel(x, c, w_ada, b_ada, norm1_g, w_in, ln_v_g, ln_v_b, w_spatial, b_spatial, w_pool,
           b_pool, pool_scale, b_gate, w_up_a, w_up_b, w_out, norm2_g, w_ff1, w_ff2, norm_f_g):
    batch, seq, d_model = x.shape
    depth = w_ada.shape[0]
    d_a = ln_v_g.shape[-1]
    d_b = pool_scale.shape[-1]
    xs = x.reshape(batch * seq, d_model)
    assert batch == 1, "sequence-local mixers are tiled assuming one sequence"
    c_col = c.reshape(d_model, 1)

    for l in range(depth):
        b_ada_row = b_ada[l].reshape(1, -1)
        mod_a, cact_b = _ada(c_col, w_ada[l], b_ada_row, 2 * d_model)
        shift1, scale1 = mod_a[:, :d_model], mod_a[:, d_model:]

        proj, w_ff1_bf, w_ff2_bf = _in_proj(xs, norm1_g[l].reshape(1, -1), scale1, shift1,
                                            w_in[l], w_ff1[l], w_ff2[l])

        bs_full = jnp.repeat(jnp.transpose(b_spatial[l]), HEAD_DIM, axis=1)
        ya, yb, w_out_bf, w_up_a_bf, w_up_b_bf = _mixer(
            proj, ln_v_g[l].reshape(1, -1), ln_v_b[l].reshape(1, -1), w_spatial[l], bs_full,
            w_pool[l].astype(BF16), b_pool[l].reshape(1, -1), pool_scale[l].reshape(1, -1),
            w_out[l], w_up_a[l], w_up_b[l], d_a, d_b)

        merged, mod_b = _merge(ya, yb, w_up_a_bf, w_up_b_bf, proj,
                               b_gate[l, 0].reshape(1, -1), b_gate[l, 1].reshape(1, -1),
                               w_ada[l], cact_b, b_ada_row, 2 * d_model, d_model)
        gate1, shift2, scale2, gate2 = [
            mod_b[:, k * d_model:(k + 1) * d_model] for k in range(4)]
        x1 = _out_proj(merged, w_out_bf, xs, gate1)

        assert depth == 1, "the final norm is fused into the last layer's channel mixer"
        xs = _ffn(x1, norm2_g[l].reshape(1, -1), scale2, shift2, gate2,
                  norm_f_g.reshape(1, -1), w_ff1_bf, w_ff2_bf)

    return xs.reshape(batch, seq, d_model)
```

```python
import functools

import jax
import jax.numpy as jnp
from jax import lax
from jax.experimental import pallas as pl
from jax.experimental.pallas import tpu as pltpu

F32 = jnp.float32
BF16 = jnp.bfloat16

EPS = 1e-6
CHUNK = 128
HEAD_DIM = 128
POOL_WINDOWS = (2, 4, 8, 16)
HALO = 16
LANES = 128
MIB = 1024 * 1024


def _params(sem, vmem_mib):
    return pltpu.CompilerParams(dimension_semantics=sem, vmem_limit_bytes=vmem_mib * MIB)


def _gelu_tanh(x):
    c = 0.7978845608028654
    half = 0.5 * x
    return half + half * jnp.tanh(x * (c + (c * 0.044715) * (x * x)))


def _matvec_cols(w_ref, cb_ref, b_ref, o_ref):
    cb = cb_ref[...]
    for n0 in range(0, o_ref.shape[1], LANES):
        cols = slice(n0, n0 + LANES)
        o_ref[:, cols] = jnp.sum(w_ref[:, cols] * cb, axis=0, keepdims=True) + b_ref[:, cols]


def _ada_kernel(c_ref, w_ref, b_ref, o_ref, cb_ref):
    @pl.when(pl.program_id(0) == 0)
    def _():
        c = c_ref[...]
        cb_ref[...] = jnp.broadcast_to(c * jax.nn.sigmoid(c), cb_ref.shape)
    _matvec_cols(w_ref, cb_ref, b_ref, o_ref)


def _ada(c_col, w_ada, b_ada, n_cols, tn=512):
    d, _ = w_ada.shape
    return pl.pallas_call(
        _ada_kernel,
        grid=(n_cols // tn,),
        in_specs=[pl.BlockSpec((d, 1), lambda j: (0, 0)),
                  pl.BlockSpec((d, tn), lambda j: (0, j)),
                  pl.BlockSpec((1, tn), lambda j: (0, j))],
        out_specs=[pl.BlockSpec((1, tn), lambda j: (0, j)),
                   pl.BlockSpec((d, LANES), lambda j: (0, 0))],
        out_shape=[jax.ShapeDtypeStruct((1, n_cols), F32),
                   jax.ShapeDtypeStruct((d, LANES), F32)],
        compiler_params=_params(("arbitrary",), 40),
        name="ada",
    )(c_col, w_ada, b_ada)


ROW_CHUNK = 16


class _RowStream:
    def __init__(self, x_hbm, row0, buf, sem, n_chunks):
        self.x_hbm, self.row0, self.buf, self.sem, self.n_chunks = x_hbm, row0, buf, sem, n_chunks
        self.slots, self.rows = buf.shape[0], buf.shape[1]

    def _copy(self, c):
        slot = c % self.slots
        src = self.x_hbm.at[pl.ds(self.row0 + c * self.rows, self.rows), :]
        return pltpu.make_async_copy(src, self.buf.at[slot], self.sem.at[slot])

    def begin(self):
        for c in range(self.slots - 1):
            self._copy(c).start()

    def next(self, c):
        self._copy(c).wait()

        @pl.when(c + self.slots - 1 < self.n_chunks)
        def _():
            self._copy(c + self.slots - 1).start()
        return c % self.slots


def _norm_mod_rows(x, w, shift):
    ms = jnp.mean(x * x, axis=-1, keepdims=True)
    return (x * lax.rsqrt(ms + EPS) * w + shift).astype(BF16)


CAST_BLK = 128


def _cast_ff_block(f1_ref, f2_ref, f1o_ref, f2o_ref):
    f1o_ref[...] = f1_ref[...].astype(BF16)
    f2o_ref[...] = f2_ref[...].astype(BF16)


def _cast_ff_specs(d, first, n_here, step_of):
    def idx(*g):
        return first + jnp.minimum(step_of(*g), n_here - 1)
    return [pl.BlockSpec((d, CAST_BLK), lambda *g: (0, idx(*g))),
            pl.BlockSpec((CAST_BLK, d), lambda *g: (idx(*g), 0))]


def _in_proj_kernel(x_hbm, g_ref, sc_ref, sh_ref, w_ref, f1_ref, f2_ref,
                    o_ref, f1o_ref, f2o_ref, h_ref, xbuf, semx, *, tm):
    i = pl.program_id(0)
    j = pl.program_id(1)
    xc = xbuf.shape[1]

    def stream_of(tile):
        return _RowStream(x_hbm, tile * tm, xbuf, semx, tm // xc)

    @pl.when((j == pl.num_programs(1) - 1) & (i + 1 < pl.num_programs(0)))
    def _():
        stream_of(i + 1).begin()

    @pl.when(j == 0)
    def _():
        w = g_ref[...] * (1.0 + sc_ref[...])
        sh = sh_ref[...]
        stream = stream_of(i)

        @pl.when(i == 0)
        def _():
            stream.begin()

        def body(c, carry):
            slot = stream.next(c)
            for r0 in range(0, xc, ROW_CHUNK):
                rows = pl.ds(pl.multiple_of(c * xc + r0, ROW_CHUNK), ROW_CHUNK)
                h_ref[rows, :] = _norm_mod_rows(xbuf[slot, r0:r0 + ROW_CHUNK, :], w, sh)
            return carry
        lax.fori_loop(0, tm // xc, body, 0)

    _cast_ff_block(f1_ref, f2_ref, f1o_ref, f2o_ref)
    o_ref[...] = jnp.dot(h_ref[...], w_ref[...].astype(BF16),
                         preferred_element_type=F32).astype(o_ref.dtype)


def _in_proj(x, g, scale, shift, w, w_ff1, w_ff2, tm=1024, tn=512, xc=128, x_slots=3):
    s, k = x.shape
    _, n = w.shape
    d, f = w_ff1.shape
    n_j = n // tn
    n_steps = (s // tm) * n_j
    n_cast = f // CAST_BLK
    assert n_cast <= n_steps <= 2 * n_cast, "every block of w_ff1 / w_ff2 must get a grid step"
    n_slow = n_steps - n_cast

    def cast_pos(i, j):
        step = i * n_j + j
        return jnp.where(step < 2 * n_slow, step // 2, step - n_slow)

    cast_specs = _cast_ff_specs(d, 0, n_cast, cast_pos)
    vec = pl.BlockSpec((1, k), lambda i, j: (0, 0))
    return pl.pallas_call(
        functools.partial(_in_proj_kernel, tm=tm),
        grid=(s // tm, n_j),
        in_specs=[pl.BlockSpec(memory_space=pl.ANY), vec, vec, vec,
                  pl.BlockSpec((k, tn), lambda i, j: (0, j))] + cast_specs,
        out_specs=[pl.BlockSpec((tm, tn), lambda i, j: (i, j))] + cast_specs,
        out_shape=[jax.ShapeDtypeStruct((s, n), BF16),
                   jax.ShapeDtypeStruct((d, f), BF16),
                   jax.ShapeDtypeStruct((f, d), BF16)],
        scratch_shapes=[pltpu.VMEM((tm, k), BF16),
                        pltpu.VMEM((x_slots, xc, k), F32),
                        pltpu.SemaphoreType.DMA((x_slots,))],
        compiler_params=_params(("arbitrary", "arbitrary"), 56),
        name="in_proj",
    )(x, g, scale, shift, w, w_ff1, w_ff2)


def _mixer_kernel(u_ref, v_ref, p_ref, halo_ref, lng_ref, lnb_ref, ws_ref, bs_ref,
                  wp_ref, bp_ref, ps_ref, wo_ref, ua_ref, ub_ref,
                  ya_ref, yb_ref, woo_ref, uao_ref, ubo_ref,
                  wsm_ref, band_ref, vn_ref, e_ref, pooled_ref, *, tm, d_a, pool_gd):
    i = pl.program_id(0)
    n_heads = d_a // HEAD_DIM
    rc = 32

    woo_ref[...] = wo_ref[...].astype(BF16)
    uao_ref[...] = ua_ref[...].astype(BF16)
    ubo_ref[...] = ub_ref[...].astype(BF16)

    @pl.when(i == 0)
    def _():
        row = lax.broadcasted_iota(jnp.int32, (CHUNK, CHUNK), 0)
        col = lax.broadcasted_iota(jnp.int32, (CHUNK, CHUNK), 1)
        for h in range(n_heads):
            wsm_ref[h] = jnp.where(row >= col, ws_ref[h], 0.0).astype(BF16)
        t = lax.broadcasted_iota(jnp.int32, (CHUNK, 2 * CHUNK), 0) + CHUNK
        s = lax.broadcasted_iota(jnp.int32, (CHUNK, 2 * CHUNK), 1)
        for g, win in enumerate(POOL_WINDOWS):
            band_ref[g] = jnp.where((s <= t) & (s > t - win), 1.0, 0.0).astype(BF16)
        e_ref[0:CHUNK - HALO, :] = jnp.zeros((CHUNK - HALO, e_ref.shape[1]), BF16)

    def ln_body(r, carry):
        rows = pl.ds(pl.multiple_of(r * rc, rc), rc)
        v = _gelu_tanh(v_ref[rows, :].astype(F32))
        mu = jnp.mean(v, axis=-1, keepdims=True)
        vc = v - mu
        var = jnp.mean(vc * vc, axis=-1, keepdims=True)
        vn = vc * lax.rsqrt(var + EPS) * lng_ref[...] + lnb_ref[...]
        vn_ref[rows, :] = vn.astype(BF16)
        return carry
    lax.fori_loop(0, tm // rc, ln_body, 0, unroll=2)

    def sgu_body(c, carry):
        rows = pl.ds(pl.multiple_of(c * CHUNK, CHUNK), CHUNK)
        for h in range(n_heads):
            cols = slice(h * HEAD_DIM, (h + 1) * HEAD_DIM)
            mixed = jnp.dot(wsm_ref[h], vn_ref[rows, cols], preferred_element_type=F32)
            ug = _gelu_tanh(u_ref[rows, cols].astype(F32))
            ya_ref[rows, cols] = (ug * (mixed + bs_ref[:, cols])).astype(ya_ref.dtype)
        return carry
    lax.fori_loop(0, tm // CHUNK, sgu_body, 0)

    e_ref[CHUNK - HALO:CHUNK, :] = jnp.where(i > 0, halo_ref[...], jnp.zeros_like(halo_ref))
    e_ref[CHUNK:, :] = p_ref[...]

    for g, win in enumerate(POOL_WINDOWS):
        cols = slice(g * pool_gd, (g + 1) * pool_gd)
        for r0 in range(0, tm, CHUNK):
            wsum = jnp.dot(band_ref[g], e_ref[r0:r0 + 2 * CHUNK, cols], preferred_element_type=F32)
            center = p_ref[r0:r0 + CHUNK, cols].astype(F32)
            t = i * tm + r0 + lax.broadcasted_iota(jnp.int32, (CHUNK, 1), 0)
            inv_cnt = 1.0 / jnp.minimum(t + 1, win).astype(F32)
            pooled_ref[r0:r0 + CHUNK, cols] = (wsum * inv_cnt - center).astype(BF16)
        y = jnp.dot(pooled_ref[:, cols], wp_ref[g], preferred_element_type=F32)
        yb_ref[:, cols] = ((y + bp_ref[:, cols]) * ps_ref[:, cols]).astype(yb_ref.dtype)


def _mixer(proj, ln_g, ln_b, w_spatial, bs_full, w_pool, b_pool, pool_scale,
           w_out, w_up_a, w_up_b, d_a, d_b, tm=256, cast_rows=128):
    s, _ = proj.shape
    d_model = w_out.shape[1]
    n_steps = s // tm
    n_heads = d_a // HEAD_DIM
    n_groups = len(POOL_WINDOWS)
    pool_gd = d_b // n_groups
    assert d_a == d_b and tm % CHUNK == 0 and tm % HALO == 0
    assert max(w_out.shape[0], w_up_a.shape[0], w_up_b.shape[0]) <= cast_rows * n_steps
    halo_blocks = tm // HALO
    vec_a = pl.BlockSpec((1, d_a), lambda i: (0, 0))
    vec_b = pl.BlockSpec((1, d_b), lambda i: (0, 0))

    def cast_spec(w):
        last = w.shape[0] // cast_rows - 1
        return pl.BlockSpec((cast_rows, w.shape[1]), lambda i: (jnp.minimum(i, last), 0))

    kern = functools.partial(_mixer_kernel, tm=tm, d_a=d_a, pool_gd=pool_gd)
    return pl.pallas_call(
        kern,
        grid=(n_steps,),
        in_specs=[
            pl.BlockSpec((tm, d_a), lambda i: (i, 0)),
            pl.BlockSpec((tm, d_a), lambda i: (i, 1)),
            pl.BlockSpec((tm, d_b), lambda i: (i, 2)),
            pl.BlockSpec((HALO, d_b), lambda i: (jnp.maximum(i * halo_blocks - 1, 0), 2)),
            vec_a, vec_a,
            pl.BlockSpec((n_heads, CHUNK, CHUNK), lambda i: (0, 0, 0)),
            pl.BlockSpec((CHUNK, d_a), lambda i: (0, 0)),
            pl.BlockSpec((n_groups, pool_gd, pool_gd), lambda i: (0, 0, 0)),
            vec_b, vec_b,
            cast_spec(w_out), cast_spec(w_up_a), cast_spec(w_up_b),
        ],
        out_specs=[pl.BlockSpec((tm, d_a), lambda i: (i, 0)),
                   pl.BlockSpec((tm, d_b), lambda i: (i, 0)),
                   cast_spec(w_out), cast_spec(w_up_a), cast_spec(w_up_b)],
        out_shape=[jax.ShapeDtypeStruct((s, d_a), BF16),
                   jax.ShapeDtypeStruct((s, d_b), BF16),
                   jax.ShapeDtypeStruct(w_out.shape, BF16),
                   jax.ShapeDtypeStruct(w_up_a.shape, BF16),
                   jax.ShapeDtypeStruct(w_up_b.shape, BF16)],
        scratch_shapes=[pltpu.VMEM((n_heads, CHUNK, CHUNK), BF16),
                        pltpu.VMEM((n_groups, CHUNK, 2 * CHUNK), BF16),
                        pltpu.VMEM((tm, d_a), BF16),
                        pltpu.VMEM((tm + CHUNK, d_b), BF16),
                        pltpu.VMEM((tm, d_b), BF16)],
        compiler_params=_params(("arbitrary",), 48),
        name="mixer",
    )(proj, proj, proj, proj, ln_g, ln_b, w_spatial, bs_full, w_pool, b_pool, pool_scale,
      w_out, w_up_a, w_up_b)


def _merge_kernel(ya_ref, yb_ref, wa_ref, wb_ref, ga_ref, gb_ref, bga_ref, bgb_ref,
                  wada_ref, cb_ref, bada_ref, o_ref, mod_ref):
    _matvec_cols(wada_ref, cb_ref, bada_ref, mod_ref)
    a = jnp.dot(ya_ref[...], wa_ref[...], preferred_element_type=F32)
    g_a = jax.nn.sigmoid(ga_ref[...].astype(F32) + bga_ref[...])
    m = g_a * a
    b = jnp.dot(yb_ref[...], wb_ref[...], preferred_element_type=F32)
    g_b = jax.nn.sigmoid(gb_ref[...].astype(F32) + bgb_ref[...])
    o_ref[...] = (m + g_b * b).astype(o_ref.dtype)


def _merge(ya, yb, wa, wb, proj, bga, bgb, w_ada, cact_b, b_ada, ada_done, d_model,
           tm=1024, tn=512):
    s, ka = ya.shape
    _, kb = yb.shape
    d, e = w_ada.shape
    ga_off = (proj.shape[1] - 2 * d_model) // tn
    gb_off = (proj.shape[1] - d_model) // tn
    n_j = d_model // tn
    n_steps = (s // tm) * n_j
    ada_blk = (e - ada_done) // n_steps
    assert ada_blk * n_steps == e - ada_done and ada_blk % LANES == 0 and ada_done % ada_blk == 0
    ada_off = ada_done // ada_blk
    return pl.pallas_call(
        _merge_kernel,
        grid=(s // tm, n_j),
        in_specs=[pl.BlockSpec((tm, ka), lambda i, j: (i, 0)),
                  pl.BlockSpec((tm, kb), lambda i, j: (i, 0)),
                  pl.BlockSpec((ka, tn), lambda i, j: (0, j)),
                  pl.BlockSpec((kb, tn), lambda i, j: (0, j)),
                  pl.BlockSpec((tm, tn), lambda i, j: (i, ga_off + j)),
                  pl.BlockSpec((tm, tn), lambda i, j: (i, gb_off + j)),
                  pl.BlockSpec((1, tn), lambda i, j: (0, j)),
                  pl.BlockSpec((1, tn), lambda i, j: (0, j)),
                  pl.BlockSpec((d, ada_blk), lambda i, j: (0, ada_off + i * n_j + j)),
                  pl.BlockSpec((d, LANES), lambda i, j: (0, 0)),
                  pl.BlockSpec((1, ada_blk), lambda i, j: (0, ada_off + i * n_j + j))],
        out_specs=[pl.BlockSpec((tm, tn), lambda i, j: (i, j)),
                   pl.BlockSpec((1, ada_blk), lambda i, j: (0, i * n_j + j))],
        out_shape=[jax.ShapeDtypeStruct((s, d_model), BF16),
                   jax.ShapeDtypeStruct((1, e - ada_done), F32)],
        compiler_params=_params(("parallel", "parallel"), 56),
        name="merge",
    )(ya, yb, wa, wb, proj, proj, bga, bgb, w_ada, cact_b, b_ada)


def _ffn_kernel(x_hbm, m_hbm, gate1_ref, wo_hbm, g2_ref, sc_ref, sh_ref, gate_ref, gf_ref,
                w1_hbm, w2_hbm, o_hbm,
                h_ref, acc_ref, r_ref, wobuf, w1buf, w2buf, obuf,
                semw, sem1, sem2, semx, semm, semo,
                *, tm, tf, n_blk, n_slots, rc, nc, oc):
    i = pl.program_id(0)
    n_tiles = pl.num_programs(0)
    base = i * n_blk
    n_out = tm // oc
    n_wo = acc_ref.shape[1] // tf
    assert n_wo % n_slots == 0

    def x_copy(tile, c):
        return pltpu.make_async_copy(x_hbm.at[pl.ds(tile * tm + c * oc, oc), :],
                                     acc_ref.at[pl.ds(c * oc, oc), :], semx.at[c])

    def m_copy(tile):
        return pltpu.make_async_copy(m_hbm.at[pl.ds(tile * tm, tm), :], h_ref, semm.at[0])

    def wo_copy(b):
        return pltpu.make_async_copy(wo_hbm.at[:, pl.ds((b % n_wo) * tf, tf)],
                                     wobuf.at[b % n_slots], semw.at[b % n_slots])

    def w1_copy(g):
        blk = g % n_blk
        slot = g % n_slots
        return pltpu.make_async_copy(w1_hbm.at[:, pl.ds(blk * tf, tf)], w1buf.at[slot], sem1.at[slot])

    def w2_copy(g):
        blk = g % n_blk
        slot = g % n_slots
        return pltpu.make_async_copy(w2_hbm.at[pl.ds(blk * tf, tf), :], w2buf.at[slot], sem2.at[slot])

    def up(g, par):
        a = jnp.dot(h_ref[...], w1buf[g % n_slots], preferred_element_type=F32)
        r = jnp.maximum(a, 0.0)
        r_ref[par] = (r * r).astype(BF16)

    def down(g, par):
        slot = g % n_slots
        for n0 in range(0, acc_ref.shape[1], nc):
            acc_ref[:, n0:n0 + nc] += gate_ref[:, n0:n0 + nc] * jnp.dot(
                r_ref[par], w2buf[slot, :, n0:n0 + nc], preferred_element_type=F32)

    @pl.when(i == 0)
    def _():
        for c in range(n_out):
            x_copy(0, c).start()
        m_copy(0).start()
        for g in range(n_slots):
            wo_copy(g).start()
            w1_copy(g).start()
            w2_copy(g).start()

    for c in range(n_out):
        x_copy(i, c).wait()
    m_copy(i).wait()
    for b in range(n_wo):
        cols = slice(b * tf, (b + 1) * tf)
        wo_copy(b).wait()
        acc_ref[:, cols] += gate1_ref[:, cols] * jnp.dot(
            h_ref[...], wobuf[b % n_slots], preferred_element_type=F32)
        wo_copy(b + n_slots).start()

    w = g2_ref[...] * (1.0 + sc_ref[...])
    sh = sh_ref[...]

    def pro_body(c, carry):
        for r0 in range(0, oc, rc):
            rows = pl.ds(pl.multiple_of(c * oc + r0, rc), rc)
            h_ref[rows, :] = _norm_mod_rows(acc_ref[rows, :], w, sh)
        return carry
    lax.fori_loop(0, n_out, pro_body, 0)

    w1_copy(base).wait()
    up(base, 0)
    w1_copy(base + n_slots).start()

    def blk_body(k, carry):
        g = base + k
        par = k % 2
        w2_copy(g).wait()
        w1_copy(g + 1).wait()
        down(g, par)
        up(g + 1, 1 - par)
        w2_copy(g + n_slots).start()
        w1_copy(g + 1 + n_slots).start()
        return carry
    lax.fori_loop(0, n_blk - 1, blk_body, 0)

    @pl.when(i + 1 < n_tiles)
    def _():
        m_copy(i + 1).start()

    g_last = base + n_blk - 1
    w2_copy(g_last).wait()
    down(g_last, (n_blk - 1) % 2)
    w2_copy(g_last + n_slots).start()

    gf = gf_ref[...]

    def out_copy(c, slot):
        return pltpu.make_async_copy(
            obuf.at[slot], o_hbm.at[pl.ds(i * tm + c * oc, oc), :], semo.at[slot])

    def epi_body(c, carry):
        slot = c % 2

        @pl.when(c >= 2)
        def _():
            out_copy(c - 2, slot).wait()

        for r0 in range(0, oc, rc):
            rows = pl.ds(pl.multiple_of(c * oc + r0, rc), rc)
            x2 = acc_ref[rows, :]
            ms = jnp.mean(x2 * x2, axis=-1, keepdims=True)
            obuf[slot, r0:r0 + rc, :] = x2 * lax.rsqrt(ms + EPS) * gf
        out_copy(c, slot).start()

        @pl.when(i + 1 < n_tiles)
        def _():
            x_copy(i + 1, c).start()
        return carry
    lax.fori_loop(0, n_out, epi_body, 0)
    out_copy(n_out - 2, n_out % 2).wait()
    out_copy(n_out - 1, (n_out - 1) % 2).wait()

    @pl.when(i == n_tiles - 1)
    def _():
        end = base + n_blk
        for g in range(n_slots):
            wo_copy(n_wo + g).wait()
            w1_copy(end + g).wait()
            w2_copy(end + g).wait()


def _ffn(x, merged, gate1, w_out, g2, scale, shift, gate, gf, w1, w2,
         tm=1024, tf=512, n_slots=2, oc=128):
    s, d = x.shape
    _, f = w1.shape
    n_blk = f // tf
    assert n_blk % 2 == 0 and n_blk > n_slots and (tm // oc) >= 2 and w_out.shape == (d, d)
    vec = pl.BlockSpec((1, d), lambda i: (0, 0))
    hbm = pl.BlockSpec(memory_space=pl.ANY)
    kern = functools.partial(_ffn_kernel, tm=tm, tf=tf, n_blk=n_blk, n_slots=n_slots,
                             rc=ROW_CHUNK, nc=1024, oc=oc)
    return pl.pallas_call(
        kern,
        grid=(s // tm,),
        in_specs=[hbm, hbm, vec, hbm, vec, vec, vec, vec, vec, hbm, hbm],
        out_specs=hbm,
        out_shape=jax.ShapeDtypeStruct((s, d), F32),
        scratch_shapes=[pltpu.VMEM((tm, d), BF16),
                        pltpu.VMEM((tm, d), F32),
                        pltpu.VMEM((2, tm, tf), BF16),
                        pltpu.VMEM((n_slots, d, tf), BF16),
                        pltpu.VMEM((n_slots, d, tf), BF16),
                        pltpu.VMEM((n_slots, tf, d), BF16),
                        pltpu.VMEM((2, oc, d), F32),
                        pltpu.SemaphoreType.DMA((n_slots,)),
                        pltpu.SemaphoreType.DMA((n_slots,)),
                        pltpu.SemaphoreType.DMA((n_slots,)),
                        pltpu.SemaphoreType.DMA((tm // oc,)),
                        pltpu.SemaphoreType.DMA((1,)),
                        pltpu.SemaphoreType.DMA((2,))],
        compiler_params=_params(("arbitrary",), 58),
        name="ffn",
    )(x, merged, gate1, w_out, g2, scale, shift, gate, gf, w1, w2)


def kernel(x, c, w_ada, b_ada, norm1_g, w_in, ln_v_g, ln_v_b, w_spatial, b_spatial, w_pool,
           b_pool, pool_scale, b_gate, w_up_a, w_up_b, w_out, norm2_g, w_ff1, w_ff2, norm_f_g):
    batch, seq, d_model = x.shape
    depth = w_ada.shape[0]
    d_a = ln_v_g.shape[-1]
    d_b = pool_scale.shape[-1]
    xs = x.reshape(batch * seq, d_model)
    assert batch == 1, "sequence-local mixers are tiled assuming one sequence"
    c_col = c.reshape(d_model, 1)

    for l in range(depth):
        b_ada_row = b_ada[l].reshape(1, -1)
        mod_a, cact_b = _ada(c_col, w_ada[l], b_ada_row, 2 * d_model)
        shift1, scale1 = mod_a[:, :d_model], mod_a[:, d_model:]

        proj, w_ff1_bf, w_ff2_bf = _in_proj(xs, norm1_g[l].reshape(1, -1), scale1, shift1,
                                            w_in[l], w_ff1[l], w_ff2[l])

        bs_full = jnp.repeat(jnp.transpose(b_spatial[l]), HEAD_DIM, axis=1)
        ya, yb, w_out_bf, w_up_a_bf, w_up_b_bf = _mixer(
            proj, ln_v_g[l].reshape(1, -1), ln_v_b[l].reshape(1, -1), w_spatial[l], bs_full,
            w_pool[l].astype(BF16), b_pool[l].reshape(1, -1), pool_scale[l].reshape(1, -1),
            w_out[l], w_up_a[l], w_up_b[l], d_a, d_b)

        merged, mod_b = _merge(ya, yb, w_up_a_bf, w_up_b_bf, proj,
                               b_gate[l, 0].reshape(1, -1), b_gate[l, 1].reshape(1, -1),
                               w_ada[l], cact_b, b_ada_row, 2 * d_model, d_model)
        gate1, shift2, scale2, gate2 = [
            mod_b[:, k * d_model:(k + 1) * d_model] for k in range(4)]

        assert depth == 1, "the final norm is fused into the last layer's channel mixer"
        xs = _ffn(xs, merged, gate1, w_out_bf, norm2_g[l].reshape(1, -1), scale2, shift2,
                  gate2, norm_f_g.reshape(1, -1), w_ff1_bf, w_ff2_bf)

    return xs.reshape(batch, seq, d_model)
```
